```python
import jax, jax.numpy as jnp
from jax import lax
import numpy as np

D_MODEL = 1024
BATCH = 2
SEQ = 16384
DEPTH = 2

CHUNK = 64
Q_BLOCK = 128
NORM_EPS = 1e-6
ROPE_THETA = 500000.0

A_HEADS = 8
A_KV_HEADS = 2
A_HEAD_DIM = 64
IDX_HEADS = 4
IDX_DIM = 64
TOPK_MAX = 256
ROT_DIM = A_HEAD_DIM // 4
B_HEADS = 8
B_HEAD_DIM = 64
B_LEFT_CHUNKS = 8
B_MAX_REL = 128
C_HEADS = 4
C_HEAD_DIM = 128
C_CONV = 4

A_WIDTH = A_HEADS * A_HEAD_DIM
A_KV_WIDTH = A_KV_HEADS * A_HEAD_DIM
B_WIDTH = B_HEADS * B_HEAD_DIM
C_WIDTH = C_HEADS * C_HEAD_DIM
BRANCH_WIDTH = 512
N_BRANCH = 3
IN_SPLITS = (A_WIDTH, A_KV_WIDTH, A_KV_WIDTH, IDX_HEADS * IDX_DIM, IDX_DIM, IDX_HEADS, A_WIDTH,
             B_WIDTH, B_WIDTH, B_WIDTH, B_WIDTH,
             C_WIDTH, C_WIDTH, C_WIDTH, C_HEADS, C_HEADS, C_WIDTH, C_WIDTH,
             N_BRANCH * D_MODEL)
IN_WIDTH = sum(IN_SPLITS)

kernel_name = 'hybrid_dsa_chunkattn_mlstm_block'


def rms_norm(x, g):
    xf = x.astype(jnp.float32)
    y = xf * lax.rsqrt(jnp.mean(xf * xf, axis=-1, keepdims=True) + NORM_EPS)
    return (y * g.astype(jnp.float32)).astype(x.dtype)


def rope_tables(S):
    half = ROT_DIM // 2
    inv_freq = 1.0 / (ROPE_THETA ** (jnp.arange(half, dtype=jnp.float32) * 2.0 / ROT_DIM))
    ang = jnp.arange(S, dtype=jnp.float32)[:, None] * inv_freq[None, :]
    return jnp.cos(ang), jnp.sin(ang)


def partial_rope(x, cos, sin):
    half = ROT_DIM // 2
    xf = x.astype(jnp.float32)
    x1 = xf[..., :half]
    x2 = xf[..., half:ROT_DIM]
    c = cos[:, None, :]
    s = sin[:, None, :]
    return jnp.concatenate([x1 * c - x2 * s, x2 * c + x1 * s, xf[..., ROT_DIM:]], axis=-1).astype(x.dtype)


def causal_depthwise_conv(x, w, b):
    K, C = w.shape
    y = lax.conv_general_dilated(x, w[:, None, :].astype(x.dtype), window_strides=(1,), padding=[(K - 1, 0)],
                                 dimension_numbers=('NWC', 'WIO', 'NWC'), feature_group_count=C)
    return y + b.astype(x.dtype)


def indexer_sparse_attention(q, k, v, iq, ik, iw):
    B, S = q.shape[0], q.shape[1]
    n_sel = min(TOPK_MAX, S // 4)
    nqb = S // Q_BLOCK
    G = A_HEADS // A_KV_HEADS
    scale = A_HEAD_DIM ** -0.5
    key_chunk = jnp.arange(S) // CHUNK
    iw = iw.astype(jnp.float32) * (IDX_DIM ** -0.5 * IDX_HEADS ** -0.5)

    def to_blocks(t):
        return jnp.swapaxes(t.reshape((B, nqb, Q_BLOCK) + t.shape[2:]), 0, 1)

    def one_block(args):
        qb, iqb, iwb, blk = args
        q_chunk = (blk * Q_BLOCK + jnp.arange(Q_BLOCK)) // CHUNK
        logits = jnp.einsum('bqhd,bsd->bqhs', iqb, ik, preferred_element_type=jnp.float32)
        score = jnp.einsum('bqhs,bqh->bqs', jax.nn.relu(logits), iwb)
        admissible = key_chunk[None, :] <= q_chunk[:, None]
        score = jnp.where(admissible[None], score, -jnp.inf)
        _, sel = lax.top_k(score, n_sel)
        sel_valid = key_chunk[sel] <= q_chunk[None, :, None]
        k_sel = jax.vmap(lambda kb, ib: kb[ib])(k, sel)
        v_sel = jax.vmap(lambda vb, ib: vb[ib])(v, sel)
        qg = qb.reshape(B, Q_BLOCK, A_KV_HEADS, G, A_HEAD_DIM)
        s = jnp.einsum('bqhgd,bqnhd->bhgqn', qg, k_sel, preferred_element_type=jnp.float32) * scale
        s = jnp.where(sel_valid[:, None, None], s, -jnp.inf)
        p = jax.nn.softmax(s, axis=-1).astype(v.dtype)
        o = jnp.einsum('bhgqn,bqnhd->bqhgd', p, v_sel)
        return o.reshape(B, Q_BLOCK, A_WIDTH)

    out = lax.map(one_block, (to_blocks(q), to_blocks(iq), to_blocks(iw), jnp.arange(nqb)))
    return jnp.swapaxes(out, 0, 1).reshape(B, S, A_WIDTH)


def chunked_relpos_attention(q, k, v, rel_bias):
    B, S = q.shape[0], q.shape[1]
    nc = S // CHUNK
    pad = B_LEFT_CHUNKS * CHUNK
    band = pad + CHUNK
    scale = B_HEAD_DIM ** -0.5
    kp = jnp.pad(k, ((0, 0), (pad, 0), (0, 0), (0, 0)))
    vp = jnp.pad(v, ((0, 0), (pad, 0), (0, 0), (0, 0)))
    rel = (pad + jnp.arange(CHUNK))[:, None] - jnp.arange(band)[None, :]
    bias = rel_bias.astype(jnp.float32)[:, jnp.clip(rel, -B_MAX_REL, B_MAX_REL) + B_MAX_REL]
    qc = jnp.swapaxes(q.reshape(B, nc, CHUNK, B_HEADS, B_HEAD_DIM), 0, 1)

    def one_chunk(args):
        qb, c = args
        start = c * CHUNK
        kb = lax.dynamic_slice_in_dim(kp, start, band, axis=1)
        vb = lax.dynamic_slice_in_dim(vp, start, band, axis=1)
        key_pos = start - pad + jnp.arange(band)
        s = jnp.einsum('bqhd,bkhd->bhqk', qb, kb, preferred_element_type=jnp.float32) * scale + bias[None]
        s = jnp.where((key_pos >= 0)[None, None, None, :], s, -jnp.inf)
        p = jax.nn.softmax(s, axis=-1).astype(v.dtype)
        o = jnp.einsum('bhqk,bkhd->bqhd', p, vb)
        return o.reshape(B, CHUNK, B_WIDTH)

    out = lax.map(one_chunk, (qc, jnp.arange(nc)))
    return jnp.swapaxes(out, 0, 1).reshape(B, S, B_WIDTH)


def mlstm_chunkwise(q, k, v, i_pre, f_pre, o_pre):
    B, S = q.shape[0], q.shape[1]
    nc = S // CHUNK
    H, d = C_HEADS, C_HEAD_DIM

    def chunks(t):
        t = t.astype(jnp.float32).reshape((B, nc, CHUNK) + t.shape[2:])
        return jnp.moveaxis(t, 3, 1)

    qc = chunks(q) * (d ** -0.5)
    kc = chunks(k)
    vc = chunks(v)
    ig = chunks(i_pre)
    log_f = jax.nn.log_sigmoid(chunks(f_pre))
    b = jnp.cumsum(log_f, axis=-1)
    g = b[..., -1]
    w_log = g[..., None] - b + ig
    m_loc = jnp.max(w_log, axis=-1)
    w = jnp.exp(w_log - m_loc[..., None])
    kv = jnp.einsum('bhcl,bhcld,bhcle->bhcde', w, kc, vc)
    ksum = jnp.einsum('bhcl,bhcld->bhcd', w, kc)

    def step(carry, xs):
        C, n, m = carry
        kv_c, ks_c, g_c, ml_c = xs
        m_new = jnp.maximum(g_c + m, ml_c)
        a = jnp.exp(g_c + m - m_new)
        bb = jnp.exp(ml_c - m_new)
        C_new = a[..., None, None] * C + bb[..., None, None] * kv_c
        n_new = a[..., None] * n + bb[..., None] * ks_c
        return (C_new, n_new, m_new), (C, n, m)

    init = (jnp.zeros((B, H, d, d), jnp.float32), jnp.zeros((B, H, d), jnp.float32), jnp.zeros((B, H), jnp.float32))
    xs = (jnp.moveaxis(kv, 2, 0), jnp.moveaxis(ksum, 2, 0), jnp.moveaxis(g, 2, 0), jnp.moveaxis(m_loc, 2, 0))
    _, (C_prev, n_prev, m_prev) = lax.scan(step, init, xs)
    C_prev = jnp.moveaxis(C_prev, 0, 2)
    n_prev = jnp.moveaxis(n_prev, 0, 2)
    m_prev = jnp.moveaxis(m_prev, 0, 2)

    causal = jnp.tril(jnp.ones((CHUNK, CHUNK), dtype=bool))
    D = jnp.where(causal, b[..., :, None] - b[..., None, :] + ig[..., None, :], -jnp.inf)
    m_inter = b + m_prev[..., None]
    m_row = jnp.maximum(m_inter, jnp.max(D, axis=-1))
    inter_scale = jnp.exp(m_inter - m_row)
    s_qk = jnp.einsum('bhcld,bhcsd->bhcls', qc, kc) * jnp.exp(D - m_row[..., None])
    num = jnp.einsum('bhcls,bhcse->bhcle', s_qk, vc) + inter_scale[..., None] * jnp.einsum('bhcld,bhcde->bhcle', qc, C_prev)
    den = jnp.sum(s_qk, axis=-1) + inter_scale * jnp.einsum('bhcld,bhcd->bhcl', qc, n_prev)
    h = num / jnp.maximum(jnp.abs(den), jnp.exp(-m_row))[..., None]
    h = jnp.moveaxis(h, 1, 3).reshape(B, S, H, d)
    return jax.nn.sigmoid(o_pre.astype(jnp.float32)) * h


def head_layer_norm(h, w):
    mu = jnp.mean(h, axis=-1, keepdims=True)
    hc = h - mu
    y = hc * lax.rsqrt(jnp.mean(hc * hc, axis=-1, keepdims=True) + NORM_EPS)
    return y.reshape(h.shape[0], h.shape[1], -1) * w.astype(jnp.float32)


def hybrid_layer(x, cos, sin, norm_g, w_in, b_in, f_bias, conv_w, conv_b, rel_bias, c_norm_w, w_branch, w_out):
    B, S, _ = x.shape
    h = rms_norm(x, norm_g)
    proj = jnp.einsum('bsd,de->bse', h, w_in) + b_in
    offsets = np.cumsum(IN_SPLITS)[:-1].tolist()
    (a_q, a_k, a_v, i_q, i_k, i_w, a_z, b_q, b_k, b_v, b_z,
     c_q, c_k, c_v, c_i, c_f, c_o, c_z, gates) = jnp.split(proj, offsets, axis=-1)
    a_q = partial_rope(a_q.reshape(B, S, A_HEADS, A_HEAD_DIM), cos, sin)
    a_k = partial_rope(a_k.reshape(B, S, A_KV_HEADS, A_HEAD_DIM), cos, sin)
    a_v = a_v.reshape(B, S, A_KV_HEADS, A_HEAD_DIM)
    i_q = partial_rope(i_q.reshape(B, S, IDX_HEADS, IDX_DIM), cos, sin)
    i_k = partial_rope(i_k.reshape(B, S, 1, IDX_DIM), cos, sin)[:, :, 0]
    y_a = indexer_sparse_attention(a_q, a_k, a_v, i_q, i_k, i_w)
    y_b = chunked_relpos_attention(b_q.reshape(B, S, B_HEADS, B_HEAD_DIM), b_k.reshape(B, S, B_HEADS, B_HEAD_DIM),
                                   b_v.reshape(B, S, B_HEADS, B_HEAD_DIM), rel_bias)
    qk = jax.nn.silu(causal_depthwise_conv(jnp.concatenate([c_q, c_k], axis=-1), conv_w, conv_b))
    c_q, c_k = jnp.split(qk, 2, axis=-1)
    y_c = mlstm_chunkwise(c_q.reshape(B, S, C_HEADS, C_HEAD_DIM), c_k.reshape(B, S, C_HEADS, C_HEAD_DIM),
                          c_v.reshape(B, S, C_HEADS, C_HEAD_DIM), c_i, c_f + f_bias,
                          c_o.reshape(B, S, C_HEADS, C_HEAD_DIM))
    y_c = head_layer_norm(y_c, c_norm_w).astype(x.dtype)
    branches = jnp.stack([y_a * jax.nn.silu(a_z), y_b * jax.nn.silu(b_z), y_c * jax.nn.silu(c_z)], axis=2)
    y = jnp.einsum('bsnw,nwd->bsnd', branches, w_branch)
    gate = jax.nn.sigmoid(gates.reshape(B, S, N_BRANCH, D_MODEL))
    merged = jnp.sum(gate * y, axis=2)
    return x + jnp.einsum('bsd,de->bse', merged, w_out)


def setup_inputs(seed: int = 0) -> dict:
    key = jax.random.key(seed)
    ks = jax.random.split(key, 13)
    f32 = jnp.float32
    x = jax.random.normal(ks[0], (BATCH, SEQ, D_MODEL), f32)
    norm_g = 1.0 + 0.01 * jax.random.normal(ks[1], (DEPTH, D_MODEL), f32)
    w_in = jax.random.normal(ks[2], (DEPTH, D_MODEL, IN_WIDTH), f32) * D_MODEL ** -0.5
    b_in = 0.01 * jax.random.normal(ks[3], (DEPTH, IN_WIDTH), f32)
    f_bias = jax.random.uniform(ks[4], (DEPTH, C_HEADS), f32, 3.0, 6.0)
    conv_w = jax.random.normal(ks[5], (DEPTH, C_CONV, 2 * C_WIDTH), f32) * C_CONV ** -0.5
    conv_b = 0.01 * jax.random.normal(ks[6], (DEPTH, 2 * C_WIDTH), f32)
    rel_bias = 0.1 * jax.random.normal(ks[7], (DEPTH, B_HEADS, 2 * B_MAX_REL + 1), f32)
    c_norm_w = 1.0 + 0.01 * jax.random.normal(ks[8], (DEPTH, C_WIDTH), f32)
    w_branch = jax.random.normal(ks[9], (DEPTH, N_BRANCH, BRANCH_WIDTH, D_MODEL), f32) * BRANCH_WIDTH ** -0.5
    w_out = jax.random.normal(ks[10], (DEPTH, D_MODEL, D_MODEL), f32) * D_MODEL ** -0.5
    final_g = 1.0 + 0.01 * jax.random.normal(ks[11], (D_MODEL,), f32)
    return {'x': x, 'norm_g': norm_g, 'w_in': w_in, 'b_in': b_in, 'f_bias': f_bias, 'conv_w': conv_w,
            'conv_b': conv_b, 'rel_bias': rel_bias, 'c_norm_w': c_norm_w, 'w_branch': w_branch,
            'w_out': w_out, 'final_g': final_g}


def reference(x, norm_g, w_in, b_in, f_bias, conv_w, conv_b, rel_bias, c_norm_w, w_branch, w_out, final_g):
    cos, sin = rope_tables(x.shape[1])
    for l in range(DEPTH):
        x = hybrid_layer(x, cos, sin, norm_g[l], w_in[l], b_in[l], f_bias[l], conv_w[l], conv_b[l],
                         rel_bias[l], c_norm_w[l], w_branch[l], w_out[l])
    return rms_norm(x, final_g)
```

```python
import functools

import numpy as np
import jax
import jax.numpy as jnp
from jax import lax
from jax.experimental import pallas as pl
from jax.experimental.pallas import tpu as pltpu

F32 = jnp.float32
BF16 = jnp.bfloat16

D_MODEL = 1024
DEPTH = 2
CHUNK = 64
CHUNK_SHIFT = 6
NORM_EPS = 1e-6
ROPE_THETA = 500000.0

A_HEADS = 8
A_KV_HEADS = 2
A_HEAD_DIM = 64
IDX_HEADS = 4
IDX_DIM = 64
TOPK_MAX = 256
ROT_DIM = A_HEAD_DIM // 4
B_HEADS = 8
B_HEAD_DIM = 64
B_LEFT_CHUNKS = 8
B_MAX_REL = 128
C_HEADS = 4
C_HEAD_DIM = 128
C_CONV = 4
N_BRANCH = 3
BRANCH_WIDTH = 512

IN_NAMES = ('a_q', 'a_k', 'a_v', 'i_q', 'i_k', 'i_w', 'a_z', 'b_q', 'b_k', 'b_v', 'b_z',
            'c_q', 'c_k', 'c_v', 'c_i', 'c_f', 'c_o', 'c_z', 'gates')
IN_SPLITS = (512, 128, 128, 256, 64, 4, 512, 512, 512, 512, 512,
             512, 512, 512, 4, 4, 512, 512, 3 * D_MODEL)
_OFF = dict(zip(IN_NAMES, np.concatenate([[0], np.cumsum(IN_SPLITS)[:-1]]).tolist()))
_WID = dict(zip(IN_NAMES, IN_SPLITS))

LANES = 128
NEG_BIG = -1e30
F32_MAX = 3.4028234663852886e38
VMEM_LIMIT = 56 * 1024 * 1024


def _cols(name):
    return np.arange(_OFF[name], _OFF[name] + _WID[name])


_ROPE_COLS = np.concatenate([_cols('a_q'), _cols('i_q'), _cols('a_k'), _cols('i_k')])
ROPE_W = 1024
_BF_COLS = np.concatenate([_cols('b_q'), _cols('b_k'), _cols('b_v'), _cols('c_v'), _cols('a_v')])
BF_W = 2176
_F32_COLS = np.concatenate([_cols('gates'), _cols('a_z'), _cols('b_z'), _cols('c_z'), _cols('c_o'),
                            _cols('c_q'), _cols('c_k'), _cols('i_w'), _cols('c_i'), _cols('c_f')])
F32_W = 6272
F32_TN = 896


def _rope_partner(cols):
    partner = cols.copy()
    within = np.arange(cols.shape[0]) % A_HEAD_DIM
    half = ROT_DIM // 2
    lo = within < half
    hi = (within >= half) & (within < ROT_DIM)
    partner[lo] = cols[np.where(lo)[0] + half]
    partner[hi] = cols[np.where(hi)[0] - half]
    return partner


_ROPE_PARTNER = _rope_partner(_ROPE_COLS)


def _pad_cols(a, width):
    return jnp.pad(a, [(0, 0)] * (a.ndim - 1) + [(0, width - a.shape[-1])])


def _normed_bf16(x_ref, g_ref):
    x = x_ref[...]
    ms = jnp.mean(x * x, axis=-1, keepdims=True)
    return (x * lax.rsqrt(ms + NORM_EPS) * g_ref[...]).astype(BF16)


def _proj_kernel(x_ref, g_ref, w_ref, b_ref, o_ref, h_scr):
    @pl.when(pl.program_id(1) == 0)
    def _():
        h_scr[...] = _normed_bf16(x_ref, g_ref)

    acc = jnp.dot(h_scr[...], w_ref[...], preferred_element_type=F32) + b_ref[...]
    o_ref[...] = acc.astype(o_ref.dtype)


def _proj_rope_kernel(x_ref, g_ref, w_ref, wp_ref, b_ref, bp_ref, c_ref, s_ref, o_ref):
    h = _normed_bf16(x_ref, g_ref)
    acc = jnp.dot(h, w_ref[...], preferred_element_type=F32) + b_ref[...]
    accp = jnp.dot(h, wp_ref[...], preferred_element_type=F32) + bp_ref[...]
    reps = acc.shape[1] // LANES
    c = pltpu.repeat(c_ref[...], reps, axis=1)
    s = pltpu.repeat(s_ref[...], reps, axis=1)
    o_ref[...] = (acc * c + accp * s).astype(o_ref.dtype)


def _proj(x2d, g, w, b, out_dtype, tn, tm=512):
    t, d = x2d.shape
    n = w.shape[1]
    return pl.pallas_call(
        _proj_kernel,
        grid=(t // tm, n // tn),
        in_specs=[
            pl.BlockSpec((tm, d), lambda i, j: (i, 0)),
            pl.BlockSpec((1, d), lambda i, j: (0, 0)),
            pl.BlockSpec((d, tn), lambda i, j: (0, j)),
            pl.BlockSpec((1, tn), lambda i, j: (0, j)),
        ],
        out_specs=pl.BlockSpec((tm, tn), lambda i, j: (i, j)),
        out_shape=jax.ShapeDtypeStruct((t, n), out_dtype),
        scratch_shapes=[pltpu.VMEM((tm, d), BF16)],
        compiler_params=pltpu.CompilerParams(
            dimension_semantics=("arbitrary", "arbitrary"), vmem_limit_bytes=VMEM_LIMIT),
        name="proj",
    )(x2d, g, w, b)


def _proj_rope(x2d, g, w, wp, b, bp, ctab, stab, seq, tm=512):
    t, d = x2d.shape
    n = w.shape[1]
    nsb = seq // tm
    return pl.pallas_call(
        _proj_rope_kernel,
        grid=(t // tm,),
        in_specs=[
            pl.BlockSpec((tm, d), lambda i: (i, 0)),
            pl.BlockSpec((1, d), lambda i: (0, 0)),
            pl.BlockSpec((d, n), lambda i: (0, 0)),
            pl.BlockSpec((d, n), lambda i: (0, 0)),
            pl.BlockSpec((1, n), lambda i: (0, 0)),
            pl.BlockSpec((1, n), lambda i: (0, 0)),
            pl.BlockSpec((tm, LANES), lambda i: (i % nsb, 0)),
            pl.BlockSpec((tm, LANES), lambda i: (i % nsb, 0)),
        ],
        out_specs=pl.BlockSpec((tm, n), lambda i: (i, 0)),
        out_shape=jax.ShapeDtypeStruct((t, n), BF16),
        compiler_params=pltpu.CompilerParams(
            dimension_semantics=("arbitrary",), vmem_limit_bytes=VMEM_LIMIT),
        name="proj_rope",
    )(x2d, g, w, wp, b, bp, ctab, stab)


def _rope_tables(seq):
    half = ROT_DIM // 2
    inv_freq = 1.0 / (ROPE_THETA ** (jnp.arange(half, dtype=F32) * 2.0 / ROT_DIM))
    ang = jnp.arange(seq, dtype=F32)[:, None] * inv_freq[None, :]
    cos, sin = jnp.cos(ang), jnp.sin(ang)
    ones = jnp.ones((seq, A_HEAD_DIM - ROT_DIM), F32)
    zeros = jnp.zeros((seq, A_HEAD_DIM - ROT_DIM), F32)
    c64 = jnp.concatenate([cos, cos, ones], axis=1)
    s64 = jnp.concatenate([-sin, sin, zeros], axis=1)
    return jnp.tile(c64, (1, LANES // A_HEAD_DIM)), jnp.tile(s64, (1, LANES // A_HEAD_DIM))


DSA_QB = 128
DSA_KT = 256
_KEY_NEG_MAX = -2139095040


def _key_to_f32(t):
    bits = jnp.where(t >= 0, t, t ^ jnp.int32(0x7FFFFFFF))
    return pltpu.bitcast(bits, F32)


def _dsa_kernel(q_ref, iq_ref, iw_ref, k_ref, ik_ref, v_ref, o_ref,
                sc_scr, iqs_scr, iwb_scr, qs_scr, m_scr, l_scr, acc_scr, *, n_sel):
    qb, kt = DSA_QB, DSA_KT
    i = pl.program_id(1)
    n_kt = ((i + 1) * qb + kt - 1) // kt
    lane = lax.broadcasted_iota(jnp.int32, (qb, LANES), 1)
    low_half = lane < A_HEAD_DIM

    for h in range(IDX_HEADS):
        slab = iq_ref[:, (h // 2) * LANES:(h // 2 + 1) * LANES].astype(F32)
        if h % 2 == 1:
            slab = pltpu.roll(slab, A_HEAD_DIM, 1)
        iqs_scr[h * qb:(h + 1) * qb, :] = jnp.where(low_half, slab, 0.0).astype(BF16)
        w_h = iw_ref[:, h:h + 1] * (IDX_DIM ** -0.5 * IDX_HEADS ** -0.5)
        iwb_scr[h * qb:(h + 1) * qb, :] = jnp.broadcast_to(w_h, (qb, LANES))
    gsz = A_HEADS // A_KV_HEADS
    for h in range(A_HEADS):
        g, hh = h // gsz, h % gsz
        slab = q_ref[:, (h // 2) * LANES:(h // 2 + 1) * LANES].astype(F32)
        if (h % 2) != g:
            slab = pltpu.roll(slab, A_HEAD_DIM, 1)
        keep = low_half if g == 0 else jnp.logical_not(low_half)
        qs_scr[g, hh * qb:(hh + 1) * qb, :] = (jnp.where(keep, slab, 0.0) * (A_HEAD_DIM ** -0.5)).astype(BF16)

    q_chunk = jnp.right_shift(i * qb + lax.broadcasted_iota(jnp.int32, (qb, kt), 0), CHUNK_SHIFT)

    def score_tile(t, masked):
        off = pl.multiple_of(t * kt, kt)
        ik_t = ik_ref[pl.ds(off, kt), :]
        lg = lax.dot_general(iqs_scr[...], ik_t, (((1,), (1,)), ((), ())), preferred_element_type=F32)
        wb = pltpu.repeat(iwb_scr[...], kt // LANES, axis=1)
        lg = jnp.maximum(lg, 0.0) * wb
        sc = (lg[0:qb] + lg[qb:2 * qb]) + (lg[2 * qb:3 * qb] + lg[3 * qb:4 * qb])
        if masked:
            k_chunk = jnp.right_shift(off + lax.broadcasted_iota(jnp.int32, (qb, kt), 1), CHUNK_SHIFT)
            sc = jnp.where(k_chunk <= q_chunk, sc, -jnp.inf)
        sc_scr[:, pl.ds(off, kt)] = sc

    def p1_body(t, c):
        score_tile(t, False)
        return c

    lax.fori_loop(0, n_kt - 1, p1_body, 0)
    score_tile(n_kt - 1, True)

    def count_ge(cand_b, strict):
        def body(t, acc):
            off = pl.multiple_of(t * kt, kt)
            tile = sc_scr[:, pl.ds(off, kt)]
            for c in range(kt // LANES):
                s = tile[:, c * LANES:(c + 1) * LANES]
                hit = (s > cand_b) if strict else (s >= cand_b)
                acc = acc + jnp.where(hit, 1, 0)
            return acc
        acc = lax.fori_loop(0, n_kt, body, jnp.zeros((qb, LANES), jnp.int32))
        return jnp.sum(acc.astype(F32), axis=1, keepdims=True)

    def bit_body(it, t):
        bit = lax.shift_left(jnp.int32(1), 31 - it)
        cand = t ^ bit
        cnt = count_ge(jnp.broadcast_to(_key_to_f32(cand), (qb, LANES)), False)
        return jnp.where(cnt >= n_sel, cand, t)

    t_key = lax.fori_loop(0, 32, bit_body, jnp.full((qb, 1), -2 ** 31, jnp.int32))
    t_key = jnp.maximum(t_key, _KEY_NEG_MAX)
    tau = _key_to_f32(t_key)
    tau_b = jnp.broadcast_to(tau, (qb, LANES))
    c_gt = count_ge(tau_b, True)
    c_ge = count_ge(tau_b, False)
    need = n_sel - c_gt
    excess = jnp.max(jnp.where(c_ge > n_sel, 1.0, 0.0))

    def attend(tie_limit_b):
        m_scr[...] = jnp.full(m_scr.shape, NEG_BIG, F32)
        l_scr[...] = jnp.zeros(l_scr.shape, F32)
        acc_scr[...] = jnp.zeros(acc_scr.shape, F32)

        def body(t, c):
            off = pl.multiple_of(t * kt, kt)
            sc = sc_scr[:, pl.ds(off, kt)]
            tb = pltpu.repeat(tau_b, kt // LANES, axis=1)
            if tie_limit_b is None:
                sel = sc >= tb
            else:
                kidx = off + lax.broadcasted_iota(jnp.int32, (qb, kt), 1)
                lim = pltpu.repeat(tie_limit_b, kt // LANES, axis=1)
                sel = (sc > tb) | ((sc == tb) & (kidx <= lim))
            k_t = k_ref[pl.ds(off, kt), :]
            v_t = v_ref[pl.ds(off, kt), :]
            for g in range(A_KV_HEADS):
                s = lax.dot_general(qs_scr[g], k_t, (((1,), (1,)), ((), ())), preferred_element_type=F32)
                s = jnp.concatenate(
                    [jnp.where(sel, s[hh * qb:(hh + 1) * qb], NEG_BIG) for hh in range(gsz)], axis=0)
                m_prev = m_scr[g]
                m_cur = jnp.max(s, axis=1, keepdims=True)
                m_next = jnp.maximum(m_prev, m_cur)
                p = jnp.exp(s - pltpu.repeat(m_next, kt // LANES, axis=1))
                alpha = jnp.exp(m_prev - m_next)
                l_scr[g] = alpha * l_scr[g] + jnp.sum(p, axis=1, keepdims=True)
                m_scr[g] = m_next
                pv = jnp.dot(p.astype(BF16), v_t, preferred_element_type=F32)
                acc_scr[g] = alpha * acc_scr[g] + pv
            return c

        lax.fori_loop(0, n_kt, body, 0)
        for j in range(A_HEADS // 2):
            halves = []
            for h in (2 * j, 2 * j + 1):
                g, hh = h // gsz, h % gsz
                rows = slice(hh * qb, (hh + 1) * qb)
                o = acc_scr[g, rows, :] / l_scr[g, rows, :]
                if (h % 2) != g:
                    o = pltpu.roll(o, A_HEAD_DIM, 1)
                halves.append(o)
            o_ref[:, j * LANES:(j + 1) * LANES] = jnp.where(low_half, halves[0], halves[1])

    @pl.when(excess == 0.0)
    def _():
        attend(None)

    @pl.when(excess != 0.0)
    def _():
        def count_ties_below(p_b):
            def body(t, acc):
                off = pl.multiple_of(t * kt, kt)
                tile = sc_scr[:, pl.ds(off, kt)]
                for c in range(kt // LANES):
                    s = tile[:, c * LANES:(c + 1) * LANES]
                    kidx = off + c * LANES + lane
                    acc = acc + jnp.where(s == tau_b, jnp.where(kidx < p_b, 1, 0), 0)
                return acc
            acc = lax.fori_loop(0, n_kt, body, jnp.zeros((qb, LANES), jnp.int32))
            return jnp.sum(acc.astype(F32), axis=1, keepdims=True)

        def idx_body(it, p):
            cand = p | lax.shift_left(jnp.int32(1), 30 - it)
            cnt = count_ties_below(jnp.broadcast_to(cand, (qb, LANES)))
            return jnp.where(cnt < need, cand, p)

        p_lim = lax.fori_loop(0, 31, idx_body, jnp.zeros((qb, 1), jnp.int32))
        attend(jnp.broadcast_to(p_lim, (qb, LANES)))


def _dsa(rope_out, bf_out, f32_out, batch, seq):
    qb = DSA_QB
    nqb = seq // qb
    n_sel = min(TOPK_MAX, seq // 4)
    gsz = A_HEADS // A_KV_HEADS
    kern = functools.partial(_dsa_kernel, n_sel=n_sel)
    return pl.pallas_call(
        kern,
        grid=(batch, nqb),
        in_specs=[
            pl.BlockSpec((qb, 512), lambda b, i: (b * nqb + i, 0)),
            pl.BlockSpec((qb, 256), lambda b, i: (b * nqb + i, 2)),
            pl.BlockSpec((qb, LANES), lambda b, i: (b * nqb + i, F32_W // LANES - 1)),
            pl.BlockSpec((seq, LANES), lambda b, i: (b, 6)),
            pl.BlockSpec((seq, LANES), lambda b, i: (b, 7)),
            pl.BlockSpec((seq, LANES), lambda b, i: (b, 16)),
        ],
        out_specs=pl.BlockSpec((qb, 512), lambda b, i: (b * nqb + i, 0)),
        out_shape=jax.ShapeDtypeStruct((batch * seq, 512), F32),
        scratch_shapes=[
            pltpu.VMEM((qb, seq), F32),
            pltpu.VMEM((IDX_HEADS * qb, LANES), BF16),
            pltpu.VMEM((IDX_HEADS * qb, LANES), F32),
            pltpu.VMEM((A_KV_HEADS, gsz * qb, LANES), BF16),
            pltpu.VMEM((A_KV_HEADS, gsz * qb, LANES), F32),
            pltpu.VMEM((A_KV_HEADS, gsz * qb, LANES), F32),
            pltpu.VMEM((A_KV_HEADS, gsz * qb, LANES), F32),
        ],
        compiler_params=pltpu.CompilerParams(
            dimension_semantics=("arbitrary", "arbitrary"), vmem_limit_bytes=VMEM_LIMIT),
        name="dsa",
    )(rope_out, rope_out, f32_out, rope_out, rope_out, bf_out)


BAND_QB = 128
BAND_KB = 5
BAND_KW = BAND_QB * BAND_KB


def _band_bias_table(rel_bias):
    r = np.arange(BAND_QB)[:, None]
    c = np.arange(BAND_KW)[None, :]
    a, j = r // CHUNK, c // CHUNK
    visible = (j >= a) & (j <= a + B_LEFT_CHUNKS)
    rel = r - c + B_LEFT_CHUNKS * CHUNK
    idx = np.clip(rel, -B_MAX_REL, B_MAX_REL) + B_MAX_REL
    tab = rel_bias.astype(F32)[:, idx]
    return jnp.where(jnp.asarray(visible)[None], tab, NEG_BIG)


def _band_kernel(q_ref, k0, k1, k2, k3, k4, v0, v1, v2, v3, v4, bias_ref, o_ref, k_scr, v_scr):
    qb, kw = BAND_QB, BAND_KW
    i = pl.program_id(1)
    for j, (kr, vr) in enumerate(((k0, v0), (k1, v1), (k2, v2), (k3, v3), (k4, v4))):
        k_scr[j * qb:(j + 1) * qb, :] = kr[...]
        v_scr[j * qb:(j + 1) * qb, :] = vr[...]
    key_pos = (i - (BAND_KB - 1)) * qb + lax.broadcasted_iota(jnp.int32, (qb, kw), 1)
    in_seq = key_pos >= 0
    lane = lax.broadcasted_iota(jnp.int32, (qb, LANES), 1)
    low_half = lane < B_HEAD_DIM
    scale = B_HEAD_DIM ** -0.5
    for j in range(B_HEADS // 2):
        sl = slice(j * LANES, (j + 1) * LANES)
        q2 = q_ref[:, sl].astype(F32)
        k2s = k_scr[:, sl]
        v2s = v_scr[:, sl]
        halves = []
        for par in range(2):
            keep = low_half if par == 0 else jnp.logical_not(low_half)
            qh = jnp.where(keep, q2, 0.0).astype(BF16)
            s = lax.dot_general(qh, k2s, (((1,), (1,)), ((), ())), preferred_element_type=F32)
            s = s * scale + bias_ref[2 * j + par]
            s = jnp.where(in_seq, s, NEG_BIG)
            m = jnp.max(s, axis=1, keepdims=True)
            p = jnp.exp(s - m)
            p = p / jnp.sum(p, axis=1, keepdims=True)
            halves.append(jnp.dot(p.astype(BF16), v2s, preferred_element_type=F32))
        o_ref[:, sl] = jnp.where(low_half, halves[0], halves[1])


def _band(bf_out, bias_tab, batch, seq):
    qb = BAND_QB
    nqb = seq // qb

    def kv_spec(col, j):
        return pl.BlockSpec(
            (qb, 512), lambda b, i: (b * nqb + jnp.maximum(i - (BAND_KB - 1) + j, 0), col))

    return pl.pallas_call(
        _band_kernel,
        grid=(batch, nqb),
        in_specs=([pl.BlockSpec((qb, 512), lambda b, i: (b * nqb + i, 0))]
                  + [kv_spec(1, j) for j in range(BAND_KB)]
                  + [kv_spec(2, j) for j in range(BAND_KB)]
                  + [pl.BlockSpec((B_HEADS, qb, BAND_KW), lambda b, i: (0, 0, 0))]),
        out_specs=pl.BlockSpec((qb, 512), lambda b, i: (b * nqb + i, 0)),
        out_shape=jax.ShapeDtypeStruct((batch * seq, 512), F32),
        scratch_shapes=[pltpu.VMEM((BAND_KW, 512), BF16), pltpu.VMEM((BAND_KW, 512), BF16)],
        compiler_params=pltpu.CompilerParams(
            dimension_semantics=("arbitrary", "arbitrary"), vmem_limit_bytes=VMEM_LIMIT),
        name="band",
    )(bf_out, *([bf_out] * (2 * BAND_KB)), bias_tab)


_IG_LANE = IDX_HEADS
_FG_LANE = IDX_HEADS + C_HEADS


def _log_sigmoid(x):
    return jnp.minimum(x, 0.0) - jnp.log1p(jnp.exp(-jnp.abs(x)))


def _mlstm_kernel(qk_ref, v_ref, o_ref, sm_ref, fb_ref, cw_ref, cb_ref, out_ref,
                  xx_scr, c_scr, n_scr, m_scr, tr_scr):
    L, d = CHUNK, C_HEAD_DIM
    c_idx = pl.program_id(1)

    @pl.when(c_idx == 0)
    def _():
        xx_scr[0:8, :] = jnp.zeros((8, xx_scr.shape[1]), F32)
        c_scr[...] = jnp.zeros(c_scr.shape, F32)
        n_scr[...] = jnp.zeros(n_scr.shape, F32)
        m_scr[...] = jnp.zeros(m_scr.shape, F32)

    x = qk_ref[...]
    xx_scr[8:8 + L, :] = x
    y = cb_ref[...]
    for j in range(C_CONV):
        y = y + cw_ref[j:j + 1, :] * xx_scr[8 - (C_CONV - 1) + j:8 - (C_CONV - 1) + j + L, :]
    xx_scr[0:8, :] = x[L - 8:L, :]
    qk = y * jax.nn.sigmoid(y)

    a = sm_ref[...] + fb_ref[...]
    logf = _log_sigmoid(a)
    row = lax.broadcasted_iota(jnp.int32, (L, L), 0)
    col = lax.broadcasted_iota(jnp.int32, (L, L), 1)
    causal = col <= row
    tri = jnp.where(causal, 1.0, 0.0).astype(F32)
    bcol = jnp.dot(tri, logf, preferred_element_type=F32, precision=lax.Precision.HIGHEST)
    lane = lax.broadcasted_iota(jnp.int32, (L, LANES), 1)
    mix = jnp.where(lane >= _FG_LANE, bcol, a)
    tr_scr[0:L, :] = mix
    tr_scr[L:2 * L, :] = jnp.zeros((L, LANES), F32)
    rows_t = jnp.transpose(tr_scr[...])

    for h in range(C_HEADS):
        sl = slice(h * d, (h + 1) * d)
        q = qk[:, sl] * (d ** -0.5)
        k = qk[:, C_HEADS * d + h * d:C_HEADS * d + (h + 1) * d]
        v = v_ref[:, sl]
        qb16 = q.astype(BF16)
        b_col = bcol[:, _FG_LANE + h:_FG_LANE + h + 1]
        i_col = a[:, _IG_LANE + h:_IG_LANE + h + 1]
        b_row = rows_t[_FG_LANE + h:_FG_LANE + h + 1, 0:L]
        i_row = rows_t[_IG_LANE + h:_IG_LANE + h + 1, 0:L]
        g = b_col[L - 1:L, :]
        m_prev = m_scr[h][:, 0:1]
        c_prev = c_scr[h]
        n_prev = n_scr[h]

        dmat = jnp.where(causal, b_col - b_row + i_row, NEG_BIG)
        m_inter = b_col + m_prev
        m_row = jnp.maximum(m_inter, jnp.max(dmat, axis=1, keepdims=True))
        inter = jnp.exp(m_inter - m_row)
        s_qk = lax.dot_general(qb16, k.astype(BF16), (((1,), (1,)), ((), ())), preferred_element_type=F32)
        s_qk = s_qk * jnp.exp(dmat - m_row)
        num = (jnp.dot(s_qk.astype(BF16), v, preferred_element_type=F32)
               + inter * jnp.dot(qb16, c_prev.astype(BF16), preferred_element_type=F32))
        den = jnp.sum(s_qk, axis=1, keepdims=True) + inter * jnp.sum(q * n_prev, axis=1, keepdims=True)
        hval = num / jnp.maximum(jnp.abs(den), jnp.exp(-m_row))
        out_ref[:, sl] = jax.nn.sigmoid(o_ref[:, sl]) * hval

        w_log = g - b_col + i_col
        m_loc = jnp.max(w_log, axis=0, keepdims=True)
        kw = k * jnp.exp(w_log - m_loc)
        kv = lax.dot_general(kw.astype(BF16), v, (((0,), (0,)), ((), ())), preferred_element_type=F32)
        ksum = jnp.sum(kw, axis=0, keepdims=True)
        m_new = jnp.maximum(g + m_prev, m_loc)
        fa = jnp.exp(g + m_prev - m_new)
        fb = jnp.exp(m_loc - m_new)
        c_scr[h] = fa * c_prev + fb * kv
        n_scr[h] = fa * n_prev + fb * ksum
        m_scr[h] = jnp.broadcast_to(m_new, (1, LANES))


def _mlstm(f32_out, bf_out, fbias_row, conv_w, conv_b, batch, seq):
    L = CHUNK
    nc = seq // L
    return pl.pallas_call(
        _mlstm_kernel,
        grid=(batch, nc),
        in_specs=[
            pl.BlockSpec((L, 1024), lambda b, c: (b * nc + c, 5)),
            pl.BlockSpec((L, 512), lambda b, c: (b * nc + c, 3)),
            pl.BlockSpec((L, 512), lambda b, c: (b * nc + c, 9)),
            pl.BlockSpec((L, LANES), lambda b, c: (b * nc + c, F32_W // LANES - 1)),
            pl.BlockSpec((1, LANES), lambda b, c: (0, 0)),
            pl.BlockSpec((C_CONV, 1024), lambda b, c: (0, 0)),
            pl.BlockSpec((1, 1024), lambda b, c: (0, 0)),
        ],
        out_specs=pl.BlockSpec((L, 512), lambda b, c: (b * nc + c, 0)),
        out_shape=jax.ShapeDtypeStruct((batch * seq, 512), F32),
        scratch_shapes=[
            pltpu.VMEM((8 + L, 1024), F32),
            pltpu.VMEM((C_HEADS, C_HEAD_DIM, C_HEAD_DIM), F32),
            pltpu.VMEM((C_HEADS, 1, C_HEAD_DIM), F32),
            pltpu.VMEM((C_HEADS, 1, LANES), F32),
            pltpu.VMEM((2 * L, LANES), F32),
        ],
        compiler_params=pltpu.CompilerParams(
            dimension_semantics=("arbitrary", "arbitrary"), vmem_limit_bytes=VMEM_LIMIT),
        name="mlstm",
    )(f32_out, bf_out, f32_out, f32_out, fbias_row, conv_w, conv_b)


def _silu(x):
    return x * jax.nn.sigmoid(x)


def _merge_kernel(x_ref, ya_ref, yb_ref, yc_ref, gate_ref, z_ref, cn_ref, wbr_ref, wout_ref, fg_ref,
                  o_ref, *, final):
    d = C_HEAD_DIM
    yc = yc_ref[...]
    parts = []
    for h in range(C_HEADS):
        hv = yc[:, h * d:(h + 1) * d]
        hc = hv - jnp.mean(hv, axis=-1, keepdims=True)
        parts.append(hc * lax.rsqrt(jnp.mean(hc * hc, axis=-1, keepdims=True) + NORM_EPS))
    yc_n = jnp.concatenate(parts, axis=1) * cn_ref[...]
    merged = None
    for n, y in enumerate((ya_ref[...], yb_ref[...], yc_n)):
        z = z_ref[:, n * BRANCH_WIDTH:(n + 1) * BRANCH_WIDTH]
        br = (y * _silu(z)).astype(BF16)
        proj = jnp.dot(br, wbr_ref[n], preferred_element_type=F32)
        term = jax.nn.sigmoid(gate_ref[:, n * D_MODEL:(n + 1) * D_MODEL]) * proj
        merged = term if merged is None else merged + term
    out = x_ref[...] + jnp.dot(merged.astype(BF16), wout_ref[...], preferred_element_type=F32)
    if final:
        ms = jnp.mean(out * out, axis=-1, keepdims=True)
        out = out * lax.rsqrt(ms + NORM_EPS) * fg_ref[...]
    o_ref[...] = out


def _merge(x2d, ya, yb, yc, f32_out, cn, wbr, wout, fg, final, tm=256):
    t, d = x2d.shape
    kern = functools.partial(_merge_kernel, final=final)
    return pl.pallas_call(
        kern,
        grid=(t // tm,),
        in_specs=[
            pl.BlockSpec((tm, d), lambda i: (i, 0)),
            pl.BlockSpec((tm, 512), lambda i: (i, 0)),
            pl.BlockSpec((tm, 512), lambda i: (i, 0)),
            pl.BlockSpec((tm, 512), lambda i: (i, 0)),
            pl.BlockSpec((tm, 3 * D_MODEL), lambda i: (i, 0)),
            pl.BlockSpec((tm, 3 * BRANCH_WIDTH), lambda i: (i, 2)),
            pl.BlockSpec((1, 512), lambda i: (0, 0)),
            pl.BlockSpec((N_BRANCH, BRANCH_WIDTH, d), lambda i: (0, 0, 0)),
            pl.BlockSpec((d, d), lambda i: (0, 0)),
            pl.BlockSpec((1, d), lambda i: (0, 0)),
        ],
        out_specs=pl.BlockSpec((tm, d), lambda i: (i, 0)),
        out_shape=jax.ShapeDtypeStruct((t, d), F32),
        compiler_params=pltpu.CompilerParams(
            dimension_semantics=("arbitrary",), vmem_limit_bytes=VMEM_LIMIT),
        name="merge",
    )(x2d, ya, yb, yc, f32_out, f32_out, cn, wbr, wout, fg)


def _layer(x2d, batch, seq, ctab, stab, norm_g, w_in, b_in, f_bias, conv_w, conv_b, rel_bias,
           c_norm_w, w_branch, w_out, final_g, final):
    g = norm_g[None, :]
    w_rope = _pad_cols(w_in[:, _ROPE_COLS], ROPE_W).astype(BF16)
    w_ropep = _pad_cols(w_in[:, _ROPE_PARTNER], ROPE_W).astype(BF16)
    b_rope = _pad_cols(b_in[_ROPE_COLS], ROPE_W)[None, :]
    b_ropep = _pad_cols(b_in[_ROPE_PARTNER], ROPE_W)[None, :]
    rope_out = _proj_rope(x2d, g, w_rope, w_ropep, b_rope, b_ropep, ctab, stab, seq)
    bf_out = _proj(x2d, g, w_in[:, _BF_COLS].astype(BF16), b_in[_BF_COLS][None, :], BF16, BF_W)
    f32_out = _proj(x2d, g, _pad_cols(w_in[:, _F32_COLS], F32_W).astype(BF16),
                    _pad_cols(b_in[_F32_COLS], F32_W)[None, :], F32, F32_TN)

    y_a = _dsa(rope_out, bf_out, f32_out, batch, seq)
    y_b = _band(bf_out, _band_bias_table(rel_bias), batch, seq)
    fbias_row = jnp.zeros((1, LANES), F32).at[0, _FG_LANE:_FG_LANE + C_HEADS].set(f_bias)
    y_c = _mlstm(f32_out, bf_out, fbias_row, conv_w, conv_b[None, :], batch, seq)
    return _merge(x2d, y_a, y_b, y_c, f32_out, c_norm_w[None, :], w_branch.astype(BF16),
                  w_out.astype(BF16), final_g[None, :], final)


def kernel(x, norm_g, w_in, b_in, f_bias, conv_w, conv_b, rel_bias, c_norm_w, w_branch, w_out, final_g):
    batch, seq, d = x.shape
    ctab, stab = _rope_tables(seq)
    x2d = x.reshape(batch * seq, d)
    depth = norm_g.shape[0]
    for l in range(depth):
        x2d = _layer(x2d, batch, seq, ctab, stab, norm_g[l], w_in[l], b_in[l], f_bias[l], conv_w[l],
                     conv_b[l], rel_bias[l], c_norm_w[l], w_branch[l], w_out[l], final_g, l == depth - 1)
    return x2d.reshape(batch, seq, d)
```

```python
import functools

import numpy as np
import jax
import jax.numpy as jnp
from jax import lax
from jax.experimental import pallas as pl
from jax.experimental.pallas import tpu as pltpu

F32 = jnp.float32
BF16 = jnp.bfloat16

D_MODEL = 1024
DEPTH = 2
CHUNK = 64
CHUNK_SHIFT = 6
NORM_EPS = 1e-6
ROPE_THETA = 500000.0

A_HEADS = 8
A_KV_HEADS = 2
A_HEAD_DIM = 64
IDX_HEADS = 4
IDX_DIM = 64
TOPK_MAX = 256
ROT_DIM = A_HEAD_DIM // 4
B_HEADS = 8
B_HEAD_DIM = 64
B_LEFT_CHUNKS = 8
B_MAX_REL = 128
C_HEADS = 4
C_HEAD_DIM = 128
C_CONV = 4
N_BRANCH = 3
BRANCH_WIDTH = 512

IN_NAMES = ('a_q', 'a_k', 'a_v', 'i_q', 'i_k', 'i_w', 'a_z', 'b_q', 'b_k', 'b_v', 'b_z',
            'c_q', 'c_k', 'c_v', 'c_i', 'c_f', 'c_o', 'c_z', 'gates')
IN_SPLITS = (512, 128, 128, 256, 64, 4, 512, 512, 512, 512, 512,
             512, 512, 512, 4, 4, 512, 512, 3 * D_MODEL)
_OFF = dict(zip(IN_NAMES, np.concatenate([[0], np.cumsum(IN_SPLITS)[:-1]]).tolist()))
_WID = dict(zip(IN_NAMES, IN_SPLITS))

LANES = 128
SUBLANES = 8
NEG_BIG = -1e30
VMEM_LIMIT = 56 * 1024 * 1024


def _cols(name):
    return np.arange(_OFF[name], _OFF[name] + _WID[name])


_ROPE_COLS = np.concatenate([_cols('a_q'), _cols('i_q'), _cols('a_k'), _cols('i_k')])
ROPE_W = 1024
_BF_COLS = np.concatenate([_cols('b_q'), _cols('b_k'), _cols('b_v'), _cols('c_v'), _cols('a_v')])
BF_W = 2176
_AV_OFF = 2048
_F32_COLS = np.concatenate([_cols('gates'), _cols('a_z'), _cols('b_z'), _cols('c_z'), _cols('c_o'),
                            _cols('c_q'), _cols('c_k'), _cols('i_w'), _cols('c_i'), _cols('c_f')])
F32_W = 6272
F32_TN = 896


def _rope_partner(cols):
    partner = cols.copy()
    within = np.arange(cols.shape[0]) % A_HEAD_DIM
    half = ROT_DIM // 2
    lo = within < half
    hi = (within >= half) & (within < ROT_DIM)
    partner[lo] = cols[np.where(lo)[0] + half]
    partner[hi] = cols[np.where(hi)[0] - half]
    return partner


_ROPE_PARTNER = _rope_partner(_ROPE_COLS)


def _pad_cols(a, width):
    return jnp.pad(a, [(0, 0)] * (a.ndim - 1) + [(0, width - a.shape[-1])])


def _normed_bf16(x_ref, g_ref):
    x = x_ref[...]
    ms = jnp.mean(x * x, axis=-1, keepdims=True)
    return (x * lax.rsqrt(ms + NORM_EPS) * g_ref[...]).astype(BF16)


def _proj_kernel(x_ref, g_ref, w_ref, b_ref, o_ref, h_scr):
    @pl.when(pl.program_id(1) == 0)
    def _():
        h_scr[...] = _normed_bf16(x_ref, g_ref)

    acc = jnp.dot(h_scr[...], w_ref[...], preferred_element_type=F32) + b_ref[...]
    o_ref[...] = acc.astype(o_ref.dtype)


def _proj_bf_kernel(x_ref, g_ref, w_ref, b_ref, o_ref, vt_ref):
    h = _normed_bf16(x_ref, g_ref)
    acc = jnp.dot(h, w_ref[...], preferred_element_type=F32) + b_ref[...]
    o_ref[...] = acc.astype(o_ref.dtype)
    for r in range(acc.shape[0] // LANES):
        blk = acc[r * LANES:(r + 1) * LANES, _AV_OFF:_AV_OFF + LANES]
        vt_ref[:, r * LANES:(r + 1) * LANES] = blk.T.astype(vt_ref.dtype)


def _proj_rope_kernel(x_ref, g_ref, w_ref, wp_ref, b_ref, bp_ref, c_ref, s_ref, o_ref):
    h = _normed_bf16(x_ref, g_ref)
    acc = jnp.dot(h, w_ref[...], preferred_element_type=F32) + b_ref[...]
    accp = jnp.dot(h, wp_ref[...], preferred_element_type=F32) + bp_ref[...]
    reps = acc.shape[1] // LANES
    c = pltpu.repeat(c_ref[...], reps, axis=1)
    s = pltpu.repeat(s_ref[...], reps, axis=1)
    o_ref[...] = (acc * c + accp * s).astype(o_ref.dtype)


def _proj(x2d, g, w, b, out_dtype, tn, tm=512):
    t, d = x2d.shape
    n = w.shape[1]
    return pl.pallas_call(
        _proj_kernel,
        grid=(t // tm, n // tn),
        in_specs=[
            pl.BlockSpec((tm, d), lambda i, j: (i, 0)),
            pl.BlockSpec((1, d), lambda i, j: (0, 0)),
            pl.BlockSpec((d, tn), lambda i, j: (0, j)),
            pl.BlockSpec((1, tn), lambda i, j: (0, j)),
        ],
        out_specs=pl.BlockSpec((tm, tn), lambda i, j: (i, j)),
        out_shape=jax.ShapeDtypeStruct((t, n), out_dtype),
        scratch_shapes=[pltpu.VMEM((tm, d), BF16)],
        compiler_params=pltpu.CompilerParams(
            dimension_semantics=("arbitrary", "arbitrary"), vmem_limit_bytes=VMEM_LIMIT),
        name="proj",
    )(x2d, g, w, b)


def _proj_bf(x2d, g, w, b, tm=512):
    t, d = x2d.shape
    n = w.shape[1]
    return pl.pallas_call(
        _proj_bf_kernel,
        grid=(t // tm,),
        in_specs=[
            pl.BlockSpec((tm, d), lambda i: (i, 0)),
            pl.BlockSpec((1, d), lambda i: (0, 0)),
            pl.BlockSpec((d, n), lambda i: (0, 0)),
            pl.BlockSpec((1, n), lambda i: (0, 0)),
        ],
        out_specs=[pl.BlockSpec((tm, n), lambda i: (i, 0)),
                   pl.BlockSpec((LANES, tm), lambda i: (0, i))],
        out_shape=[jax.ShapeDtypeStruct((t, n), BF16), jax.ShapeDtypeStruct((LANES, t), BF16)],
        compiler_params=pltpu.CompilerParams(
            dimension_semantics=("arbitrary",), vmem_limit_bytes=VMEM_LIMIT),
        name="proj_bf",
    )(x2d, g, w, b)


def _proj_rope(x2d, g, w, wp, b, bp, ctab, stab, seq, tm=512):
    t, d = x2d.shape
    n = w.shape[1]
    nsb = seq // tm
    return pl.pallas_call(
        _proj_rope_kernel,
        grid=(t // tm,),
        in_specs=[
            pl.BlockSpec((tm, d), lambda i: (i, 0)),
            pl.BlockSpec((1, d), lambda i: (0, 0)),
            pl.BlockSpec((d, n), lambda i: (0, 0)),
            pl.BlockSpec((d, n), lambda i: (0, 0)),
            pl.BlockSpec((1, n), lambda i: (0, 0)),
            pl.BlockSpec((1, n), lambda i: (0, 0)),
            pl.BlockSpec((tm, LANES), lambda i: (i % nsb, 0)),
            pl.BlockSpec((tm, LANES), lambda i: (i % nsb, 0)),
        ],
        out_specs=pl.BlockSpec((tm, n), lambda i: (i, 0)),
        out_shape=jax.ShapeDtypeStruct((t, n), BF16),
        compiler_params=pltpu.CompilerParams(
            dimension_semantics=("arbitrary",), vmem_limit_bytes=VMEM_LIMIT),
        name="proj_rope",
    )(x2d, g, w, wp, b, bp, ctab, stab)


def _rope_tables(seq):
    half = ROT_DIM // 2
    inv_freq = 1.0 / (ROPE_THETA ** (jnp.arange(half, dtype=F32) * 2.0 / ROT_DIM))
    ang = jnp.arange(seq, dtype=F32)[:, None] * inv_freq[None, :]
    cos, sin = jnp.cos(ang), jnp.sin(ang)
    ones = jnp.ones((seq, A_HEAD_DIM - ROT_DIM), F32)
    zeros = jnp.zeros((seq, A_HEAD_DIM - ROT_DIM), F32)
    c64 = jnp.concatenate([cos, cos, ones], axis=1)
    s64 = jnp.concatenate([-sin, sin, zeros], axis=1)
    return jnp.tile(c64, (1, LANES // A_HEAD_DIM)), jnp.tile(s64, (1, LANES // A_HEAD_DIM))


DSA_QB = 128
DSA_KT = 512
_KEY_NEG_MAX = -2139095040
_GSZ = A_HEADS // A_KV_HEADS


def _key_to_f32(t):
    bits = jnp.where(t >= 0, t, t ^ jnp.int32(0x7FFFFFFF))
    return pltpu.bitcast(bits, F32)


def _dsa_kernel(q_ref, iq_ref, iw_ref, k_ref, ik_ref, vt_ref, o_ref,
                sc_scr, iqt_scr, wrow_scr, qt_scr, m_scr, l_scr, acc_scr, *, n_sel):
    qb, kt = DSA_QB, DSA_KT
    nq4 = _GSZ * qb
    i = pl.program_id(1)
    n_kt = ((i + 1) * qb + kt - 1) // kt
    lane = lax.broadcasted_iota(jnp.int32, (qb, LANES), 1)
    low_half = lane < A_HEAD_DIM

    for h in range(IDX_HEADS):
        slab = iq_ref[:, (h // 2) * LANES:(h // 2 + 1) * LANES].astype(F32)
        if h % 2 == 1:
            slab = pltpu.roll(slab, A_HEAD_DIM, 1)
        iqt_scr[:, h * qb:(h + 1) * qb] = jnp.where(low_half, slab, 0.0).T.astype(BF16)
    iw_t = iw_ref[...].T
    for h in range(IDX_HEADS):
        wrow_scr[0:1, h * qb:(h + 1) * qb] = iw_t[h:h + 1, :] * (IDX_DIM ** -0.5 * IDX_HEADS ** -0.5)
    for h in range(A_HEADS):
        g, hh = h // _GSZ, h % _GSZ
        slab = q_ref[:, (h // 2) * LANES:(h // 2 + 1) * LANES].astype(F32)
        if (h % 2) != g:
            slab = pltpu.roll(slab, A_HEAD_DIM, 1)
        keep = low_half if g == 0 else jnp.logical_not(low_half)
        slab = jnp.where(keep, slab, 0.0) * (A_HEAD_DIM ** -0.5)
        qt_scr[g, :, hh * qb:(hh + 1) * qb] = slab.T.astype(BF16)

    q_chunk = jnp.right_shift(i * qb + lax.broadcasted_iota(jnp.int32, (kt, qb), 1), CHUNK_SHIFT)

    def score_tile(t, masked):
        off = pl.multiple_of(t * kt, kt)
        lg = jnp.dot(ik_ref[pl.ds(off, kt), :], iqt_scr[...], preferred_element_type=F32)
        lg = jnp.maximum(lg, 0.0) * wrow_scr[0:1, :]
        sc = (lg[:, 0:qb] + lg[:, qb:2 * qb]) + (lg[:, 2 * qb:3 * qb] + lg[:, 3 * qb:4 * qb])
        if masked:
            k_chunk = jnp.right_shift(off + lax.broadcasted_iota(jnp.int32, (kt, qb), 0), CHUNK_SHIFT)
            sc = jnp.where(k_chunk <= q_chunk, sc, -jnp.inf)
        sc_scr[pl.ds(off, kt), :] = sc

    def p1_body(t, c):
        score_tile(t, False)
        return c

    lax.fori_loop(0, n_kt - 1, p1_body, 0)
    score_tile(n_kt - 1, True)

    n_acc = 4
    zero_acc = tuple(jnp.zeros((SUBLANES, qb), jnp.int32) for _ in range(n_acc))

    def count_rows(hit_fn):
        def body(t, accs):
            off = pl.multiple_of(t * kt, kt)
            accs = list(accs)
            for r in range(kt // SUBLANES):
                s = sc_scr[pl.ds(off + r * SUBLANES, SUBLANES), :]
                accs[r % n_acc] = accs[r % n_acc] + jnp.where(hit_fn(s, off + r * SUBLANES), 1, 0)
            return tuple(accs)
        accs = lax.fori_loop(0, n_kt, body, zero_acc)
        tot = (accs[0] + accs[1]) + (accs[2] + accs[3])
        return jnp.sum(tot.astype(F32), axis=0, keepdims=True)

    def bit_body(it, t):
        bit = lax.shift_left(jnp.int32(1), 31 - it)
        cand = t ^ bit
        cand_f = _key_to_f32(cand)
        cnt = count_rows(lambda s, r0: s >= cand_f)
        return jnp.where(cnt >= n_sel, cand, t)

    t_key = lax.fori_loop(0, 32, bit_body, jnp.full((1, qb), -2 ** 31, jnp.int32))
    t_key = jnp.maximum(t_key, _KEY_NEG_MAX)
    tau = _key_to_f32(t_key)
    c_gt = count_rows(lambda s, r0: s > tau)
    c_ge = count_rows(lambda s, r0: s >= tau)
    need = n_sel - c_gt
    excess = jnp.max(jnp.where(c_ge > n_sel, 1.0, 0.0))

    def attend(tie_limit):
        m_scr[...] = jnp.full(m_scr.shape, NEG_BIG, F32)
        l_scr[...] = jnp.zeros(l_scr.shape, F32)
        acc_scr[...] = jnp.zeros(acc_scr.shape, F32)

        def body(t, c):
            off = pl.multiple_of(t * kt, kt)
            sc = sc_scr[pl.ds(off, kt), :]
            if tie_limit is None:
                sel = sc >= tau
            else:
                kidx = off + lax.broadcasted_iota(jnp.int32, (kt, qb), 0)
                sel = (sc > tau) | ((sc == tau) & (kidx <= tie_limit))
            k_t = k_ref[pl.ds(off, kt), :]
            for g in range(A_KV_HEADS):
                s = jnp.dot(k_t, qt_scr[g], preferred_element_type=F32)
                s = jnp.concatenate(
                    [jnp.where(sel, s[:, hh * qb:(hh + 1) * qb], NEG_BIG) for hh in range(_GSZ)], axis=1)
                m_prev = m_scr[g]
                m_next = jnp.maximum(m_prev, jnp.max(s, axis=0, keepdims=True))
                p = jnp.exp(s - m_next)
                alpha = jnp.exp(m_prev - m_next)
                l_scr[g] = alpha * l_scr[g] + jnp.sum(p, axis=0, keepdims=True)
                m_scr[g] = m_next
                vt_g = vt_ref[g * A_HEAD_DIM:(g + 1) * A_HEAD_DIM, pl.ds(off, kt)]
                pv = jnp.dot(vt_g, p.astype(BF16), preferred_element_type=F32)
                acc_scr[g] = alpha * acc_scr[g] + pv
            return c

        lax.fori_loop(0, n_kt, body, 0)
        for j in range(A_HEADS // 2):
            g, hh0 = j // 2, 2 * (j % 2)
            o_t = acc_scr[g] / l_scr[g]
            pair = jnp.concatenate([o_t[:, hh0 * qb:(hh0 + 1) * qb],
                                    o_t[:, (hh0 + 1) * qb:(hh0 + 2) * qb]], axis=0)
            o_ref[:, j * LANES:(j + 1) * LANES] = pair.T

    @pl.when(excess == 0.0)
    def _():
        attend(None)

    @pl.when(excess != 0.0)
    def _():
        row = lax.broadcasted_iota(jnp.int32, (SUBLANES, qb), 0)

        def idx_body(it, p):
            cand = p | lax.shift_left(jnp.int32(1), 30 - it)
            cnt = count_rows(lambda s, r0: (s == tau) & (r0 + row < cand))
            return jnp.where(cnt < need, cand, p)

        attend(lax.fori_loop(0, 31, idx_body, jnp.zeros((1, qb), jnp.int32)))


def _dsa(rope_out, av_t, f32_out, batch, seq):
    qb = DSA_QB
    nqb = seq // qb
    n_sel = min(TOPK_MAX, seq // 4)
    nq4 = _GSZ * qb
    kern = functools.partial(_dsa_kernel, n_sel=n_sel)
    return pl.pallas_call(
        kern,
        grid=(batch, nqb),
        in_specs=[
            pl.BlockSpec((qb, 512), lambda b, i: (b * nqb + i, 0)),
            pl.BlockSpec((qb, 256), lambda b, i: (b * nqb + i, 2)),
            pl.BlockSpec((qb, LANES), lambda b, i: (b * nqb + i, F32_W // LANES - 1)),
            pl.BlockSpec((seq, LANES), lambda b, i: (b, 6)),
            pl.BlockSpec((seq, LANES), lambda b, i: (b, 7)),
            pl.BlockSpec((LANES, seq), lambda b, i: (0, b)),
        ],
        out_specs=pl.BlockSpec((qb, 512), lambda b, i: (b * nqb + i, 0)),
        out_shape=jax.ShapeDtypeStruct((batch * seq, 512), F32),
        scratch_shapes=[
            pltpu.VMEM((seq, qb), F32),
            pltpu.VMEM((LANES, IDX_HEADS * qb), BF16),
            pltpu.VMEM((SUBLANES, IDX_HEADS * qb), F32),
            pltpu.VMEM((A_KV_HEADS, LANES, nq4), BF16),
            pltpu.VMEM((A_KV_HEADS, 1, nq4), F32),
            pltpu.VMEM((A_KV_HEADS, 1, nq4), F32),
            pltpu.VMEM((A_KV_HEADS, A_HEAD_DIM, nq4), F32),
        ],
        compiler_params=pltpu.CompilerParams(
            dimension_semantics=("arbitrary", "arbitrary"), vmem_limit_bytes=VMEM_LIMIT),
        name="dsa",
    )(rope_out, rope_out, f32_out, rope_out, rope_out, av_t)


BAND_QB = 128
BAND_KB = 5
BAND_KW = BAND_QB * BAND_KB


def _band_bias_table(rel_bias):
    r = np.arange(BAND_QB)[:, None]
    c = np.arange(BAND_KW)[None, :]
    a, j = r // CHUNK, c // CHUNK
    visible = (j >= a) & (j <= a + B_LEFT_CHUNKS)
    rel = r - c + B_LEFT_CHUNKS * CHUNK
    idx = np.clip(rel, -B_MAX_REL, B_MAX_REL) + B_MAX_REL
    tab = rel_bias.astype(F32)[:, idx]
    return jnp.where(jnp.asarray(visible)[None], tab, NEG_BIG)


def _band_kernel(q_ref, k0, k1, k2, k3, k4, v0, v1, v2, v3, v4, bias_ref, o_ref, k_scr, v_scr):
    qb, kw = BAND_QB, BAND_KW
    i = pl.program_id(1)
    for j, (kr, vr) in enumerate(((k0, v0), (k1, v1), (k2, v2), (k3, v3), (k4, v4))):
        k_scr[j * qb:(j + 1) * qb, :] = kr[...]
        v_scr[j * qb:(j + 1) * qb, :] = vr[...]
    key_pos = (i - (BAND_KB - 1)) * qb + lax.broadcasted_iota(jnp.int32, (qb, kw), 1)
    in_seq = key_pos >= 0
    lane = lax.broadcasted_iota(jnp.int32, (qb, LANES), 1)
    low_half = lane < B_HEAD_DIM
    scale = B_HEAD_DIM ** -0.5
    for j in range(B_HEADS // 2):
        sl = slice(j * LANES, (j + 1) * LANES)
        q2 = q_ref[:, sl].astype(F32)
        k2s = k_scr[:, sl]
        v2s = v_scr[:, sl]
        halves = []
        for par in range(2):
            keep = low_half if par == 0 else jnp.logical_not(low_half)
            qh = jnp.where(keep, q2, 0.0).astype(BF16)
            s = lax.dot_general(qh, k2s, (((1,), (1,)), ((), ())), preferred_element_type=F32)
            s = s * scale + bias_ref[2 * j + par]
            s = jnp.where(in_seq, s, NEG_BIG)
            m = jnp.max(s, axis=1, keepdims=True)
            p = jnp.exp(s - m)
            p = p / jnp.sum(p, axis=1, keepdims=True)
            halves.append(jnp.dot(p.astype(BF16), v2s, preferred_element_type=F32))
        o_ref[:, sl] = jnp.where(low_half, halves[0], halves[1])


def _band(bf_out, bias_tab, batch, seq):
    qb = BAND_QB
    nqb = seq // qb

    def kv_spec(col, j):
        return pl.BlockSpec(
            (qb, 512), lambda b, i: (b * nqb + jnp.maximum(i - (BAND_KB - 1) + j, 0), col))

    return pl.pallas_call(
        _band_kernel,
        grid=(batch, nqb),
        in_specs=([pl.BlockSpec((qb, 512), lambda b, i: (b * nqb + i, 0))]
                  + [kv_spec(1, j) for j in range(BAND_KB)]
                  + [kv_spec(2, j) for j in range(BAND_KB)]
                  + [pl.BlockSpec((B_HEADS, qb, BAND_KW), lambda b, i: (0, 0, 0))]),
        out_specs=pl.BlockSpec((qb, 512), lambda b, i: (b * nqb + i, 0)),
        out_shape=jax.ShapeDtypeStruct((batch * seq, 512), F32),
        scratch_shapes=[pltpu.VMEM((BAND_KW, 512), BF16), pltpu.VMEM((BAND_KW, 512), BF16)],
        compiler_params=pltpu.CompilerParams(
            dimension_semantics=("arbitrary", "arbitrary"), vmem_limit_bytes=VMEM_LIMIT),
        name="band",
    )(bf_out, *([bf_out] * (2 * BAND_KB)), bias_tab)


_IG_LANE = IDX_HEADS
_FG_LANE = IDX_HEADS + C_HEADS


def _log_sigmoid(x):
    return jnp.minimum(x, 0.0) - jnp.log1p(jnp.exp(-jnp.abs(x)))


def _mlstm_kernel(qk_ref, v_ref, o_ref, sm_ref, fb_ref, cw_ref, cb_ref, out_ref,
                  xx_scr, c_scr, n_scr, m_scr, tr_scr):
    L, d = CHUNK, C_HEAD_DIM
    c_idx = pl.program_id(1)

    @pl.when(c_idx == 0)
    def _():
        xx_scr[0:8, :] = jnp.zeros((8, xx_scr.shape[1]), F32)
        c_scr[...] = jnp.zeros(c_scr.shape, F32)
        n_scr[...] = jnp.zeros(n_scr.shape, F32)
        m_scr[...] = jnp.zeros(m_scr.shape, F32)

    x = qk_ref[...]
    xx_scr[8:8 + L, :] = x
    y = cb_ref[...]
    for j in range(C_CONV):
        y = y + cw_ref[j:j + 1, :] * xx_scr[8 - (C_CONV - 1) + j:8 - (C_CONV - 1) + j + L, :]
    xx_scr[0:8, :] = x[L - 8:L, :]
    qk = y * jax.nn.sigmoid(y)

    a = sm_ref[...] + fb_ref[...]
    logf = _log_sigmoid(a)
    row = lax.broadcasted_iota(jnp.int32, (L, L), 0)
    col = lax.broadcasted_iota(jnp.int32, (L, L), 1)
    causal = col <= row
    tri = jnp.where(causal, 1.0, 0.0).astype(F32)
    bcol = jnp.dot(tri, logf, preferred_element_type=F32, precision=lax.Precision.HIGHEST)
    lane = lax.broadcasted_iota(jnp.int32, (L, LANES), 1)
    mix = jnp.where(lane >= _FG_LANE, bcol, a)
    tr_scr[0:L, :] = mix
    tr_scr[L:2 * L, :] = jnp.zeros((L, LANES), F32)
    rows_t = jnp.transpose(tr_scr[...])

    for h in range(C_HEADS):
        sl = slice(h * d, (h + 1) * d)
        q = qk[:, sl] * (d ** -0.5)
        k = qk[:, C_HEADS * d + h * d:C_HEADS * d + (h + 1) * d]
        v = v_ref[:, sl]
        qb16 = q.astype(BF16)
        b_col = bcol[:, _FG_LANE + h:_FG_LANE + h + 1]
        i_col = a[:, _IG_LANE + h:_IG_LANE + h + 1]
        b_row = rows_t[_FG_LANE + h:_FG_LANE + h + 1, 0:L]
        i_row = rows_t[_IG_LANE + h:_IG_LANE + h + 1, 0:L]
        g = b_col[L - 1:L, :]
        m_prev = m_scr[h][:, 0:1]
        c_prev = c_scr[h]
        n_prev = n_scr[h]

        dmat = jnp.where(causal, b_col - b_row + i_row, NEG_BIG)
        m_inter = b_col + m_prev
        m_row = jnp.maximum(m_inter, jnp.max(dmat, axis=1, keepdims=True))
        inter = jnp.exp(m_inter - m_row)
        s_qk = lax.dot_general(qb16, k.astype(BF16), (((1,), (1,)), ((), ())), preferred_element_type=F32)
        s_qk = s_qk * jnp.exp(dmat - m_row)
        num = (jnp.dot(s_qk.astype(BF16), v, preferred_element_type=F32)
               + inter * jnp.dot(qb16, c_prev.astype(BF16), preferred_element_type=F32))
        den = jnp.sum(s_qk, axis=1, keepdims=True) + inter * jnp.sum(q * n_prev, axis=1, keepdims=True)
        hval = num / jnp.maximum(jnp.abs(den), jnp.exp(-m_row))
        out_ref[:, sl] = jax.nn.sigmoid(o_ref[:, sl]) * hval

        w_log = g - b_col + i_col
        m_loc = jnp.max(w_log, axis=0, keepdims=True)
        kw = k * jnp.exp(w_log - m_loc)
        kv = lax.dot_general(kw.astype(BF16), v, (((0,), (0,)), ((), ())), preferred_element_type=F32)
        ksum = jnp.sum(kw, axis=0, keepdims=True)
        m_new = jnp.maximum(g + m_prev, m_loc)
        fa = jnp.exp(g + m_prev - m_new)
        fb = jnp.exp(m_loc - m_new)
        c_scr[h] = fa * c_prev + fb * kv
        n_scr[h] = fa * n_prev + fb * ksum
        m_scr[h] = jnp.broadcast_to(m_new, (1, LANES))


def _mlstm(f32_out, bf_out, fbias_row, conv_w, conv_b, batch, seq):
    L = CHUNK
    nc = seq // L
    return pl.pallas_call(
        _mlstm_kernel,
        grid=(batch, nc),
        in_specs=[
            pl.BlockSpec((L, 1024), lambda b, c: (b * nc + c, 5)),
            pl.BlockSpec((L, 512), lambda b, c: (b * nc + c, 3)),
            pl.BlockSpec((L, 512), lambda b, c: (b * nc + c, 9)),
            pl.BlockSpec((L, LANES), lambda b, c: (b * nc + c, F32_W // LANES - 1)),
            pl.BlockSpec((1, LANES), lambda b, c: (0, 0)),
            pl.BlockSpec((C_CONV, 1024), lambda b, c: (0, 0)),
            pl.BlockSpec((1, 1024), lambda b, c: (0, 0)),
        ],
        out_specs=pl.BlockSpec((L, 512), lambda b, c: (b * nc + c, 0)),
        out_shape=jax.ShapeDtypeStruct((batch * seq, 512), F32),
        scratch_shapes=[
            pltpu.VMEM((8 + L, 1024), F32),
            pltpu.VMEM((C_HEADS, C_HEAD_DIM, C_HEAD_DIM), F32),
            pltpu.VMEM((C_HEADS, 1, C_HEAD_DIM), F32),
            pltpu.VMEM((C_HEADS, 1, LANES), F32),
            pltpu.VMEM((2 * L, LANES), F32),
        ],
        compiler_params=pltpu.CompilerParams(
            dimension_semantics=("arbitrary", "arbitrary"), vmem_limit_bytes=VMEM_LIMIT),
        name="mlstm",
    )(f32_out, bf_out, f32_out, f32_out, fbias_row, conv_w, conv_b)


def _silu(x):
    return x * jax.nn.sigmoid(x)


def _merge_kernel(x_ref, ya_ref, yb_ref, yc_ref, gate_ref, z_ref, cn_ref, wbr_ref, wout_ref, fg_ref,
                  o_ref, *, final):
    d = C_HEAD_DIM
    yc = yc_ref[...]
    parts = []
    for h in range(C_HEADS):
        hv = yc[:, h * d:(h + 1) * d]
        hc = hv - jnp.mean(hv, axis=-1, keepdims=True)
        parts.append(hc * lax.rsqrt(jnp.mean(hc * hc, axis=-1, keepdims=True) + NORM_EPS))
    yc_n = jnp.concatenate(parts, axis=1) * cn_ref[...]
    merged = None
    for n, y in enumerate((ya_ref[...], yb_ref[...], yc_n)):
        z = z_ref[:, n * BRANCH_WIDTH:(n + 1) * BRANCH_WIDTH]
        br = (y * _silu(z)).astype(BF16)
        proj = jnp.dot(br, wbr_ref[n], preferred_element_type=F32)
        term = jax.nn.sigmoid(gate_ref[:, n * D_MODEL:(n + 1) * D_MODEL]) * proj
        merged = term if merged is None else merged + term
    out = x_ref[...] + jnp.dot(merged.astype(BF16), wout_ref[...], preferred_element_type=F32)
    if final:
        ms = jnp.mean(out * out, axis=-1, keepdims=True)
        out = out * lax.rsqrt(ms + NORM_EPS) * fg_ref[...]
    o_ref[...] = out


def _merge(x2d, ya, yb, yc, f32_out, cn, wbr, wout, fg, final, tm=256):
    t, d = x2d.shape
    kern = functools.partial(_merge_kernel, final=final)
    return pl.pallas_call(
        kern,
        grid=(t // tm,),
        in_specs=[
            pl.BlockSpec((tm, d), lambda i: (i, 0)),
            pl.BlockSpec((tm, 512), lambda i: (i, 0)),
            pl.BlockSpec((tm, 512), lambda i: (i, 0)),
            pl.BlockSpec((tm, 512), lambda i: (i, 0)),
            pl.BlockSpec((tm, 3 * D_MODEL), lambda i: (i, 0)),
            pl.BlockSpec((tm, 3 * BRANCH_WIDTH), lambda i: (i, 2)),
            pl.BlockSpec((1, 512), lambda i: (0, 0)),
            pl.BlockSpec((N_BRANCH, BRANCH_WIDTH, d), lambda i: (0, 0, 0)),
            pl.BlockSpec((d, d), lambda i: (0, 0)),
            pl.BlockSpec((1, d), lambda i: (0, 0)),
        ],
        out_specs=pl.BlockSpec((tm, d), lambda i: (i, 0)),
        out_shape=jax.ShapeDtypeStruct((t, d), F32),
        compiler_params=pltpu.CompilerParams(
            dimension_semantics=("arbitrary",), vmem_limit_bytes=VMEM_LIMIT),
        name="merge",
    )(x2d, ya, yb, yc, f32_out, f32_out, cn, wbr, wout, fg)


def _layer(x2d, batch, seq, ctab, stab, norm_g, w_in, b_in, f_bias, conv_w, conv_b, rel_bias,
           c_norm_w, w_branch, w_out, final_g, final):
    g = norm_g[None, :]
    w_rope = _pad_cols(w_in[:, _ROPE_COLS], ROPE_W).astype(BF16)
    w_ropep = _pad_cols(w_in[:, _ROPE_PARTNER], ROPE_W).astype(BF16)
    b_rope = _pad_cols(b_in[_ROPE_COLS], ROPE_W)[None, :]
    b_ropep = _pad_cols(b_in[_ROPE_PARTNER], ROPE_W)[None, :]
    rope_out = _proj_rope(x2d, g, w_rope, w_ropep, b_rope, b_ropep, ctab, stab, seq)
    bf_out, av_t = _proj_bf(x2d, g, w_in[:, _BF_COLS].astype(BF16), b_in[_BF_COLS][None, :])
    f32_out = _proj(x2d, g, _pad_cols(w_in[:, _F32_COLS], F32_W).astype(BF16),
                    _pad_cols(b_in[_F32_COLS], F32_W)[None, :], F32, F32_TN)

    y_a = _dsa(rope_out, av_t, f32_out, batch, seq)
    y_b = _band(bf_out, _band_bias_table(rel_bias), batch, seq)
    fbias_row = jnp.zeros((1, LANES), F32).at[0, _FG_LANE:_FG_LANE + C_HEADS].set(f_bias)
    y_c = _mlstm(f32_out, bf_out, fbias_row, conv_w, conv_b[None, :], batch, seq)
    return _merge(x2d, y_a, y_b, y_c, f32_out, c_norm_w[None, :], w_branch.astype(BF16),
                  w_out.astype(BF16), final_g[None, :], final)


def kernel(x, norm_g, w_in, b_in, f_bias, conv_w, conv_b, rel_bias, c_norm_w, w_branch, w_out, final_g):
    batch, seq, d = x.shape
    ctab, stab = _rope_tables(seq)
    x2d = x.reshape(batch * seq, d)
    depth = norm_g.shape[0]
    for l in range(depth):
        x2d = _layer(x2d, batch, seq, ctab, stab, norm_g[l], w_in[l], b_in[l], f_bias[l], conv_w[l],
                     conv_b[l], rel_bias[l], c_norm_w[l], w_branch[l], w_out[l], final_g, l == depth - 1)
    return x2d.reshape(batch, seq, d)
```

```python
import functools

import numpy as np
import jax
import jax.numpy as jnp
from jax import lax
from jax.experimental import pallas as pl
from jax.experimental.pallas import tpu as pltpu

F32 = jnp.float32
BF16 = jnp.bfloat16

D_MODEL = 1024
DEPTH = 2
CHUNK = 64
CHUNK_SHIFT = 6
NORM_EPS = 1e-6
ROPE_THETA = 500000.0

A_HEADS = 8
A_KV_HEADS = 2
A_HEAD_DIM = 64
IDX_HEADS = 4
IDX_DIM = 64
TOPK_MAX = 256
ROT_DIM = A_HEAD_DIM // 4
B_HEADS = 8
B_HEAD_DIM = 64
B_LEFT_CHUNKS = 8
B_MAX_REL = 128
C_HEADS = 4
C_HEAD_DIM = 128
C_CONV = 4
N_BRANCH = 3
BRANCH_WIDTH = 512

IN_NAMES = ('a_q', 'a_k', 'a_v', 'i_q', 'i_k', 'i_w', 'a_z', 'b_q', 'b_k', 'b_v', 'b_z',
            'c_q', 'c_k', 'c_v', 'c_i', 'c_f', 'c_o', 'c_z', 'gates')
IN_SPLITS = (512, 128, 128, 256, 64, 4, 512, 512, 512, 512, 512,
             512, 512, 512, 4, 4, 512, 512, 3 * D_MODEL)
_OFF = dict(zip(IN_NAMES, np.concatenate([[0], np.cumsum(IN_SPLITS)[:-1]]).tolist()))
_WID = dict(zip(IN_NAMES, IN_SPLITS))

LANES = 128
SUBLANES = 8
NEG_BIG = -1e30
VMEM_LIMIT = 56 * 1024 * 1024


def _cols(name):
    return np.arange(_OFF[name], _OFF[name] + _WID[name])


_ROPE_COLS = np.concatenate([_cols('a_q'), _cols('i_q'), _cols('a_k'), _cols('i_k')])
ROPE_W = 1024
_BF_COLS = np.concatenate([_cols('b_q'), _cols('b_k'), _cols('b_v'), _cols('c_v'), _cols('a_v')])
BF_W = 2176
_AV_OFF = 2048
_F32_COLS = np.concatenate([_cols('gates'), _cols('a_z'), _cols('b_z'), _cols('c_z'), _cols('c_o'),
                            _cols('c_q'), _cols('c_k'), _cols('i_w'), _cols('c_i'), _cols('c_f')])
F32_W = 6272
F32_TN = 896


def _rope_partner(cols):
    partner = cols.copy()
    within = np.arange(cols.shape[0]) % A_HEAD_DIM
    half = ROT_DIM // 2
    lo = within < half
    hi = (within >= half) & (within < ROT_DIM)
    partner[lo] = cols[np.where(lo)[0] + half]
    partner[hi] = cols[np.where(hi)[0] - half]
    return partner


_ROPE_PARTNER = _rope_partner(_ROPE_COLS)


def _pad_cols(a, width):
    return jnp.pad(a, [(0, 0)] * (a.ndim - 1) + [(0, width - a.shape[-1])])


def _normed_bf16(x_ref, g_ref):
    x = x_ref[...]
    ms = jnp.mean(x * x, axis=-1, keepdims=True)
    return (x * lax.rsqrt(ms + NORM_EPS) * g_ref[...]).astype(BF16)


def _proj_kernel(x_ref, g_ref, w_ref, b_ref, o_ref, h_scr):
    @pl.when(pl.program_id(1) == 0)
    def _():
        h_scr[...] = _normed_bf16(x_ref, g_ref)

    acc = jnp.dot(h_scr[...], w_ref[...], preferred_element_type=F32) + b_ref[...]
    o_ref[...] = acc.astype(o_ref.dtype)


def _proj_bf_kernel(x_ref, g_ref, w_ref, b_ref, o_ref, vt_ref):
    h = _normed_bf16(x_ref, g_ref)
    acc = jnp.dot(h, w_ref[...], preferred_element_type=F32) + b_ref[...]
    o_ref[...] = acc.astype(o_ref.dtype)
    for r in range(acc.shape[0] // LANES):
        blk = acc[r * LANES:(r + 1) * LANES, _AV_OFF:_AV_OFF + LANES]
        vt_ref[:, r * LANES:(r + 1) * LANES] = blk.T.astype(vt_ref.dtype)


def _proj_rope_kernel(x_ref, g_ref, w_ref, wp_ref, b_ref, bp_ref, c_ref, s_ref, o_ref):
    h = _normed_bf16(x_ref, g_ref)
    acc = jnp.dot(h, w_ref[...], preferred_element_type=F32) + b_ref[...]
    accp = jnp.dot(h, wp_ref[...], preferred_element_type=F32) + bp_ref[...]
    reps = acc.shape[1] // LANES
    c = pltpu.repeat(c_ref[...], reps, axis=1)
    s = pltpu.repeat(s_ref[...], reps, axis=1)
    o_ref[...] = (acc * c + accp * s).astype(o_ref.dtype)


def _proj(x2d, g, w, b, out_dtype, tn, tm=512):
    t, d = x2d.shape
    n = w.shape[1]
    return pl.pallas_call(
        _proj_kernel,
        grid=(t // tm, n // tn),
        in_specs=[
            pl.BlockSpec((tm, d), lambda i, j: (i, 0)),
            pl.BlockSpec((1, d), lambda i, j: (0, 0)),
            pl.BlockSpec((d, tn), lambda i, j: (0, j)),
            pl.BlockSpec((1, tn), lambda i, j: (0, j)),
        ],
        out_specs=pl.BlockSpec((tm, tn), lambda i, j: (i, j)),
        out_shape=jax.ShapeDtypeStruct((t, n), out_dtype),
        scratch_shapes=[pltpu.VMEM((tm, d), BF16)],
        compiler_params=pltpu.CompilerParams(
            dimension_semantics=("arbitrary", "arbitrary"), vmem_limit_bytes=VMEM_LIMIT),
        name="proj",
    )(x2d, g, w, b)


def _proj_bf(x2d, g, w, b, tm=512):
    t, d = x2d.shape
    n = w.shape[1]
    return pl.pallas_call(
        _proj_bf_kernel,
        grid=(t // tm,),
        in_specs=[
            pl.BlockSpec((tm, d), lambda i: (i, 0)),
            pl.BlockSpec((1, d), lambda i: (0, 0)),
            pl.BlockSpec((d, n), lambda i: (0, 0)),
            pl.BlockSpec((1, n), lambda i: (0, 0)),
        ],
        out_specs=[pl.BlockSpec((tm, n), lambda i: (i, 0)),
                   pl.BlockSpec((LANES, tm), lambda i: (0, i))],
        out_shape=[jax.ShapeDtypeStruct((t, n), BF16), jax.ShapeDtypeStruct((LANES, t), BF16)],
        compiler_params=pltpu.CompilerParams(
            dimension_semantics=("arbitrary",), vmem_limit_bytes=VMEM_LIMIT),
        name="proj_bf",
    )(x2d, g, w, b)


def _proj_rope(x2d, g, w, wp, b, bp, ctab, stab, seq, tm=512):
    t, d = x2d.shape
    n = w.shape[1]
    nsb = seq // tm
    return pl.pallas_call(
        _proj_rope_kernel,
        grid=(t // tm,),
        in_specs=[
            pl.BlockSpec((tm, d), lambda i: (i, 0)),
            pl.BlockSpec((1, d), lambda i: (0, 0)),
            pl.BlockSpec((d, n), lambda i: (0, 0)),
            pl.BlockSpec((d, n), lambda i: (0, 0)),
            pl.BlockSpec((1, n), lambda i: (0, 0)),
            pl.BlockSpec((1, n), lambda i: (0, 0)),
            pl.BlockSpec((tm, LANES), lambda i: (i % nsb, 0)),
            pl.BlockSpec((tm, LANES), lambda i: (i % nsb, 0)),
        ],
        out_specs=pl.BlockSpec((tm, n), lambda i: (i, 0)),
        out_shape=jax.ShapeDtypeStruct((t, n), BF16),
        compiler_params=pltpu.CompilerParams(
            dimension_semantics=("arbitrary",), vmem_limit_bytes=VMEM_LIMIT),
        name="proj_rope",
    )(x2d, g, w, wp, b, bp, ctab, stab)


def _rope_tables(seq):
    half = ROT_DIM // 2
    inv_freq = 1.0 / (ROPE_THETA ** (jnp.arange(half, dtype=F32) * 2.0 / ROT_DIM))
    ang = jnp.arange(seq, dtype=F32)[:, None] * inv_freq[None, :]
    cos, sin = jnp.cos(ang), jnp.sin(ang)
    ones = jnp.ones((seq, A_HEAD_DIM - ROT_DIM), F32)
    zeros = jnp.zeros((seq, A_HEAD_DIM - ROT_DIM), F32)
    c64 = jnp.concatenate([cos, cos, ones], axis=1)
    s64 = jnp.concatenate([-sin, sin, zeros], axis=1)
    return jnp.tile(c64, (1, LANES // A_HEAD_DIM)), jnp.tile(s64, (1, LANES // A_HEAD_DIM))


DSA_QB = 128
DSA_KT = 512
_KEY_NEG_MAX = -2139095040
_GSZ = A_HEADS // A_KV_HEADS


def _key_to_f32(t):
    bits = jnp.where(t >= 0, t, t ^ jnp.int32(0x7FFFFFFF))
    return pltpu.bitcast(bits, F32)


def _dsa_kernel(q_ref, iq_ref, iw_ref, k_ref, ik_ref, vt_ref, o_ref,
                sc_scr, iqt_scr, wrow_scr, wq_scr, ltri_scr, sa_scr, sb_scr, m_scr, l_scr, acc_scr,
                *, n_sel):
    qb, kt = DSA_QB, DSA_KT
    nq4 = _GSZ * qb
    i = pl.program_id(1)
    n_kt = ((i + 1) * qb + kt - 1) // kt
    lane = lax.broadcasted_iota(jnp.int32, (qb, LANES), 1)
    low_half = lane < A_HEAD_DIM

    for h in range(IDX_HEADS):
        slab = iq_ref[:, (h // 2) * LANES:(h // 2 + 1) * LANES].astype(F32)
        if h % 2 == 1:
            slab = pltpu.roll(slab, A_HEAD_DIM, 1)
        iqt_scr[:, h * qb:(h + 1) * qb] = jnp.where(low_half, slab, 0.0).T.astype(BF16)
    iw_t = iw_ref[...].T
    for h in range(IDX_HEADS):
        wrow_scr[0:1, h * qb:(h + 1) * qb] = iw_t[h:h + 1, :] * (IDX_DIM ** -0.5 * IDX_HEADS ** -0.5)
    eye = jnp.where(lax.broadcasted_iota(jnp.int32, (qb, LANES), 0) == lane, 1.0, 0.0).astype(BF16)
    for h in range(A_HEADS):
        g, hh = h // _GSZ, h % _GSZ
        slab = q_ref[:, (h // 2) * LANES:(h // 2 + 1) * LANES].astype(F32)
        if (h % 2) != g:
            slab = pltpu.roll(slab, A_HEAD_DIM, 1)
        keep = low_half if g == 0 else jnp.logical_not(low_half)
        slab = jnp.where(keep, slab, 0.0) * (A_HEAD_DIM ** -0.5)
        wq_scr[g, 0:LANES, hh * qb:(hh + 1) * qb] = slab.T.astype(BF16)
        wq_scr[g, LANES:2 * LANES, hh * qb:(hh + 1) * qb] = eye
    tri_r = lax.broadcasted_iota(jnp.int32, (kt, kt), 0)
    tri_c = lax.broadcasted_iota(jnp.int32, (kt, kt), 1)
    ltri_scr[...] = jnp.where(tri_c < tri_r, 1.0, 0.0).astype(BF16)

    q_chunk = jnp.right_shift(i * qb + lax.broadcasted_iota(jnp.int32, (kt, qb), 1), CHUNK_SHIFT)

    def score_tile(t, masked):
        off = pl.multiple_of(t * kt, kt)
        lg = jnp.dot(ik_ref[pl.ds(off, kt), :], iqt_scr[...], preferred_element_type=F32)
        lg = jnp.maximum(lg, 0.0) * wrow_scr[0:1, :]
        sc = (lg[:, 0:qb] + lg[:, qb:2 * qb]) + (lg[:, 2 * qb:3 * qb] + lg[:, 3 * qb:4 * qb])
        if masked:
            k_chunk = jnp.right_shift(off + lax.broadcasted_iota(jnp.int32, (kt, qb), 0), CHUNK_SHIFT)
            sc = jnp.where(k_chunk <= q_chunk, sc, -jnp.inf)
        sc_scr[pl.ds(off, kt), :] = sc

    def p1_body(t, c):
        score_tile(t, False)
        return c

    lax.fori_loop(0, n_kt - 1, p1_body, 0)
    score_tile(n_kt - 1, True)

    n_acc = 4
    zero_acc = tuple(jnp.zeros((SUBLANES, qb), jnp.int32) for _ in range(n_acc))

    def count_rows(hit_fn):
        def body(t, accs):
            off = pl.multiple_of(t * kt, kt)
            accs = list(accs)
            for r in range(kt // SUBLANES):
                s = sc_scr[pl.ds(off + r * SUBLANES, SUBLANES), :]
                accs[r % n_acc] = accs[r % n_acc] + jnp.where(hit_fn(s), 1, 0)
            return tuple(accs)
        accs = lax.fori_loop(0, n_kt, body, zero_acc)
        tot = (accs[0] + accs[1]) + (accs[2] + accs[3])
        return jnp.sum(tot.astype(F32), axis=0, keepdims=True)

    def bit_body(it, t):
        bit = lax.shift_left(jnp.int32(1), 31 - it)
        cand = t ^ bit
        cand_f = _key_to_f32(cand)
        cnt = count_rows(lambda s: s >= cand_f)
        return jnp.where(cnt >= n_sel, cand, t)

    t_key = lax.fori_loop(0, 32, bit_body, jnp.full((1, qb), -2 ** 31, jnp.int32))
    t_key = jnp.maximum(t_key, _KEY_NEG_MAX)
    tau = _key_to_f32(t_key)
    c_gt = count_rows(lambda s: s > tau)
    need = n_sel - c_gt

    m_scr[...] = jnp.full(m_scr.shape, NEG_BIG, F32)
    l_scr[...] = jnp.zeros(l_scr.shape, F32)
    acc_scr[...] = jnp.zeros(acc_scr.shape, F32)

    def tile_offset(t):
        return pl.multiple_of(jnp.minimum(t, n_kt - 1) * kt, kt)

    def tie_ranks(t, ties_before):
        sc = sc_scr[pl.ds(tile_offset(t), kt), :]
        tie_f = jnp.where(sc == tau, 1.0, 0.0)
        rank = jnp.dot(ltri_scr[...], tie_f.astype(BF16), preferred_element_type=F32) + ties_before
        return sc, tie_f, rank

    def masked_keys(t, sc, tie_f, rank):
        sel = (sc > tau) | ((tie_f > 0.0) & (rank < need))
        open_bias = jnp.where(t < n_kt, 0.0, NEG_BIG)
        bias = jnp.where(sel, open_bias, NEG_BIG).astype(BF16)
        return jnp.concatenate([k_ref[pl.ds(tile_offset(t), kt), :], bias], axis=1)

    def accumulate(t, src, g):
        m_prev = m_scr[g]
        m_next = jnp.maximum(m_prev, jnp.max(src[g], axis=0, keepdims=True))
        alpha = jnp.exp(m_prev - m_next)
        p = jnp.exp(src[g] - m_next)
        l_scr[g] = alpha * l_scr[g] + jnp.sum(p, axis=0, keepdims=True)
        m_scr[g] = m_next
        vt_g = vt_ref[g * A_HEAD_DIM:(g + 1) * A_HEAD_DIM, pl.ds(tile_offset(t), kt)]
        pv = jnp.dot(vt_g, p.astype(BF16), preferred_element_type=F32)
        acc_scr[g] = alpha * acc_scr[g] + pv

    def step(t, ties, src, dst):
        sc, tie_f, rank = tie_ranks(t + 1, ties)
        accumulate(t, src, 0)
        lhs = masked_keys(t + 1, sc, tie_f, rank)
        dst[0] = jnp.dot(lhs, wq_scr[0], preferred_element_type=F32)
        accumulate(t, src, 1)
        dst[1] = jnp.dot(lhs, wq_scr[1], preferred_element_type=F32)
        return rank[kt - 1:kt, :] + tie_f[kt - 1:kt, :]

    def tile_pair(u, ties):
        ties = step(2 * u, ties, sa_scr, sb_scr)
        return step(2 * u + 1, ties, sb_scr, sa_scr)

    zero_ties = jnp.zeros((1, qb), F32)
    sc0, tie0, rank0 = tie_ranks(0, zero_ties)
    lhs0 = masked_keys(0, sc0, tie0, rank0)
    for g in range(A_KV_HEADS):
        sa_scr[g] = jnp.dot(lhs0, wq_scr[g], preferred_element_type=F32)
    lax.fori_loop(0, (n_kt + 1) // 2, tile_pair, rank0[kt - 1:kt, :] + tie0[kt - 1:kt, :])
    for j in range(A_HEADS // 2):
        g, hh0 = j // 2, 2 * (j % 2)
        o_t = acc_scr[g] / l_scr[g]
        pair = jnp.concatenate([o_t[:, hh0 * qb:(hh0 + 1) * qb],
                                o_t[:, (hh0 + 1) * qb:(hh0 + 2) * qb]], axis=0)
        o_ref[:, j * LANES:(j + 1) * LANES] = pair.T


def _dsa(rope_out, av_t, f32_out, batch, seq):
    qb = DSA_QB
    nqb = seq // qb
    n_sel = min(TOPK_MAX, seq // 4)
    nq4 = _GSZ * qb
    kern = functools.partial(_dsa_kernel, n_sel=n_sel)
    return pl.pallas_call(
        kern,
        grid=(batch, nqb),
        in_specs=[
            pl.BlockSpec((qb, 512), lambda b, i: (b * nqb + i, 0)),
            pl.BlockSpec((qb, 256), lambda b, i: (b * nqb + i, 2)),
            pl.BlockSpec((qb, LANES), lambda b, i: (b * nqb + i, F32_W // LANES - 1)),
            pl.BlockSpec((seq, LANES), lambda b, i: (b, 6)),
            pl.BlockSpec((seq, LANES), lambda b, i: (b, 7)),
            pl.BlockSpec((LANES, seq), lambda b, i: (0, b)),
        ],
        out_specs=pl.BlockSpec((qb, 512), lambda b, i: (b * nqb + i, 0)),
        out_shape=jax.ShapeDtypeStruct((batch * seq, 512), F32),
        scratch_shapes=[
            pltpu.VMEM((seq, qb), F32),
            pltpu.VMEM((LANES, IDX_HEADS * qb), BF16),
            pltpu.VMEM((SUBLANES, IDX_HEADS * qb), F32),
            pltpu.VMEM((A_KV_HEADS, 2 * LANES, nq4), BF16),
            pltpu.VMEM((DSA_KT, DSA_KT), BF16),
            pltpu.VMEM((A_KV_HEADS, DSA_KT, nq4), F32),
            pltpu.VMEM((A_KV_HEADS, DSA_KT, nq4), F32),
            pltpu.VMEM((A_KV_HEADS, 1, nq4), F32),
            pltpu.VMEM((A_KV_HEADS, 1, nq4), F32),
            pltpu.VMEM((A_KV_HEADS, A_HEAD_DIM, nq4), F32),
        ],
        compiler_params=pltpu.CompilerParams(
            dimension_semantics=("arbitrary", "arbitrary"), vmem_limit_bytes=VMEM_LIMIT),
        name="dsa",
    )(rope_out, rope_out, f32_out, rope_out, rope_out, av_t)


BAND_QB = 128
BAND_KB = 5
BAND_KW = BAND_QB * BAND_KB


def _band_bias_table(rel_bias):
    r = np.arange(BAND_QB)[:, None]
    c = np.arange(BAND_KW)[None, :]
    a, j = r // CHUNK, c // CHUNK
    visible = (j >= a) & (j <= a + B_LEFT_CHUNKS)
    rel = r - c + B_LEFT_CHUNKS * CHUNK
    idx = np.clip(rel, -B_MAX_REL, B_MAX_REL) + B_MAX_REL
    tab = rel_bias.astype(F32)[:, idx]
    return jnp.where(jnp.asarray(visible)[None], tab, NEG_BIG)


def _band_kernel(q_ref, k0, k1, k2, k3, k4, v0, v1, v2, v3, v4, bias_ref, o_ref, k_scr, v_scr):
    qb, kw = BAND_QB, BAND_KW
    i = pl.program_id(1)
    for j, (kr, vr) in enumerate(((k0, v0), (k1, v1), (k2, v2), (k3, v3), (k4, v4))):
        k_scr[j * qb:(j + 1) * qb, :] = kr[...]
        v_scr[j * qb:(j + 1) * qb, :] = vr[...]
    key_pos = (i - (BAND_KB - 1)) * qb + lax.broadcasted_iota(jnp.int32, (qb, kw), 1)
    in_seq = key_pos >= 0
    lane = lax.broadcasted_iota(jnp.int32, (qb, LANES), 1)
    low_half = lane < B_HEAD_DIM
    scale = B_HEAD_DIM ** -0.5
    for j in range(B_HEADS // 2):
        sl = slice(j * LANES, (j + 1) * LANES)
        q2 = q_ref[:, sl].astype(F32)
        k2s = k_scr[:, sl]
        v2s = v_scr[:, sl]
        halves = []
        for par in range(2):
            keep = low_half if par == 0 else jnp.logical_not(low_half)
            qh = jnp.where(keep, q2, 0.0).astype(BF16)
            s = lax.dot_general(qh, k2s, (((1,), (1,)), ((), ())), preferred_element_type=F32)
            s = s * scale + bias_ref[2 * j + par]
            s = jnp.where(in_seq, s, NEG_BIG)
            m = jnp.max(s, axis=1, keepdims=True)
            p = jnp.exp(s - m)
            p = p / jnp.sum(p, axis=1, keepdims=True)
            halves.append(jnp.dot(p.astype(BF16), v2s, preferred_element_type=F32))
        o_ref[:, sl] = jnp.where(low_half, halves[0], halves[1])


def _band(bf_out, bias_tab, batch, seq):
    qb = BAND_QB
    nqb = seq // qb

    def kv_spec(col, j):
        return pl.BlockSpec(
            (qb, 512), lambda b, i: (b * nqb + jnp.maximum(i - (BAND_KB - 1) + j, 0), col))

    return pl.pallas_call(
        _band_kernel,
        grid=(batch, nqb),
        in_specs=([pl.BlockSpec((qb, 512), lambda b, i: (b * nqb + i, 0))]
                  + [kv_spec(1, j) for j in range(BAND_KB)]
                  + [kv_spec(2, j) for j in range(BAND_KB)]
                  + [pl.BlockSpec((B_HEADS, qb, BAND_KW), lambda b, i: (0, 0, 0))]),
        out_specs=pl.BlockSpec((qb, 512), lambda b, i: (b * nqb + i, 0)),
        out_shape=jax.ShapeDtypeStruct((batch * seq, 512), F32),
        scratch_shapes=[pltpu.VMEM((BAND_KW, 512), BF16), pltpu.VMEM((BAND_KW, 512), BF16)],
        compiler_params=pltpu.CompilerParams(
            dimension_semantics=("arbitrary", "arbitrary"), vmem_limit_bytes=VMEM_LIMIT),
        name="band",
    )(bf_out, *([bf_out] * (2 * BAND_KB)), bias_tab)


_IG_LANE = IDX_HEADS
_FG_LANE = IDX_HEADS + C_HEADS


def _log_sigmoid(x):
    return jnp.minimum(x, 0.0) - jnp.log1p(jnp.exp(-jnp.abs(x)))


def _mlstm_kernel(qk_ref, v_ref, o_ref, sm_ref, fb_ref, cw_ref, cb_ref, out_ref,
                  xx_scr, c_scr, n_scr, m_scr, tr_scr):
    L, d = CHUNK, C_HEAD_DIM
    c_idx = pl.program_id(1)

    @pl.when(c_idx == 0)
    def _():
        xx_scr[0:8, :] = jnp.zeros((8, xx_scr.shape[1]), F32)
        c_scr[...] = jnp.zeros(c_scr.shape, F32)
        n_scr[...] = jnp.zeros(n_scr.shape, F32)
        m_scr[...] = jnp.zeros(m_scr.shape, F32)

    x = qk_ref[...]
    xx_scr[8:8 + L, :] = x
    y = cb_ref[...]
    for j in range(C_CONV):
        y = y + cw_ref[j:j + 1, :] * xx_scr[8 - (C_CONV - 1) + j:8 - (C_CONV - 1) + j + L, :]
    xx_scr[0:8, :] = x[L - 8:L, :]
    qk = y * jax.nn.sigmoid(y)

    a = sm_ref[...] + fb_ref[...]
    logf = _log_sigmoid(a)
    row = lax.broadcasted_iota(jnp.int32, (L, L), 0)
    col = lax.broadcasted_iota(jnp.int32, (L, L), 1)
    causal = col <= row
    tri = jnp.where(causal, 1.0, 0.0).astype(F32)
    bcol = jnp.dot(tri, logf, preferred_element_type=F32, precision=lax.Precision.HIGHEST)
    lane = lax.broadcasted_iota(jnp.int32, (L, LANES), 1)
    mix = jnp.where(lane >= _FG_LANE, bcol, a)
    tr_scr[0:L, :] = mix
    tr_scr[L:2 * L, :] = jnp.zeros((L, LANES), F32)
    rows_t = jnp.transpose(tr_scr[...])

    for h in range(C_HEADS):
        sl = slice(h * d, (h + 1) * d)
        q = qk[:, sl] * (d ** -0.5)
        k = qk[:, C_HEADS * d + h * d:C_HEADS * d + (h + 1) * d]
        v = v_ref[:, sl]
        qb16 = q.astype(BF16)
        b_col = bcol[:, _FG_LANE + h:_FG_LANE + h + 1]
        i_col = a[:, _IG_LANE + h:_IG_LANE + h + 1]
        b_row = rows_t[_FG_LANE + h:_FG_LANE + h + 1, 0:L]
        i_row = rows_t[_IG_LANE + h:_IG_LANE + h + 1, 0:L]
        g = b_col[L - 1:L, :]
        m_prev = m_scr[h][:, 0:1]
        c_prev = c_scr[h]
        n_prev = n_scr[h]

        dmat = jnp.where(causal, b_col - b_row + i_row, NEG_BIG)
        m_inter = b_col + m_prev
        m_row = jnp.maximum(m_inter, jnp.max(dmat, axis=1, keepdims=True))
        inter = jnp.exp(m_inter - m_row)
        s_qk = lax.dot_general(qb16, k.astype(BF16), (((1,), (1,)), ((), ())), preferred_element_type=F32)
        s_qk = s_qk * jnp.exp(dmat - m_row)
        num = (jnp.dot(s_qk.astype(BF16), v, preferred_element_type=F32)
               + inter * jnp.dot(qb16, c_prev.astype(BF16), preferred_element_type=F32))
        den = jnp.sum(s_qk, axis=1, keepdims=True) + inter * jnp.sum(q * n_prev, axis=1, keepdims=True)
        hval = num / jnp.maximum(jnp.abs(den), jnp.exp(-m_row))
        out_ref[:, sl] = jax.nn.sigmoid(o_ref[:, sl]) * hval

        w_log = g - b_col + i_col
        m_loc = jnp.max(w_log, axis=0, keepdims=True)
        kw = k * jnp.exp(w_log - m_loc)
        kv = lax.dot_general(kw.astype(BF16), v, (((0,), (0,)), ((), ())), preferred_element_type=F32)
        ksum = jnp.sum(kw, axis=0, keepdims=True)
        m_new = jnp.maximum(g + m_prev, m_loc)
        fa = jnp.exp(g + m_prev - m_new)
        fb = jnp.exp(m_loc - m_new)
        c_scr[h] = fa * c_prev + fb * kv
        n_scr[h] = fa * n_prev + fb * ksum
        m_scr[h] = jnp.broadcast_to(m_new, (1, LANES))


def _mlstm(f32_out, bf_out, fbias_row, conv_w, conv_b, batch, seq):
    L = CHUNK
    nc = seq // L
    return pl.pallas_call(
        _mlstm_kernel,
        grid=(batch, nc),
        in_specs=[
            pl.BlockSpec((L, 1024), lambda b, c: (b * nc + c, 5)),
            pl.BlockSpec((L, 512), lambda b, c: (b * nc + c, 3)),
            pl.BlockSpec((L, 512), lambda b, c: (b * nc + c, 9)),
            pl.BlockSpec((L, LANES), lambda b, c: (b * nc + c, F32_W // LANES - 1)),
            pl.BlockSpec((1, LANES), lambda b, c: (0, 0)),
            pl.BlockSpec((C_CONV, 1024), lambda b, c: (0, 0)),
            pl.BlockSpec((1, 1024), lambda b, c: (0, 0)),
        ],
        out_specs=pl.BlockSpec((L, 512), lambda b, c: (b * nc + c, 0)),
        out_shape=jax.ShapeDtypeStruct((batch * seq, 512), F32),
        scratch_shapes=[
            pltpu.VMEM((8 + L, 1024), F32),
            pltpu.VMEM((C_HEADS, C_HEAD_DIM, C_HEAD_DIM), F32),
            pltpu.VMEM((C_HEADS, 1, C_HEAD_DIM), F32),
            pltpu.VMEM((C_HEADS, 1, LANES), F32),
            pltpu.VMEM((2 * L, LANES), F32),
        ],
        compiler_params=pltpu.CompilerParams(
            dimension_semantics=("arbitrary", "arbitrary"), vmem_limit_bytes=VMEM_LIMIT),
        name="mlstm",
    )(f32_out, bf_out, f32_out, f32_out, fbias_row, conv_w, conv_b)


def _silu(x):
    return x * jax.nn.sigmoid(x)


def _merge_kernel(x_ref, ya_ref, yb_ref, yc_ref, gate_ref, z_ref, cn_ref, wbr_ref, wout_ref, fg_ref,
                  o_ref, *, final):
    d = C_HEAD_DIM
    yc = yc_ref[...]
    parts = []
    for h in range(C_HEADS):
        hv = yc[:, h * d:(h + 1) * d]
        hc = hv - jnp.mean(hv, axis=-1, keepdims=True)
        parts.append(hc * lax.rsqrt(jnp.mean(hc * hc, axis=-1, keepdims=True) + NORM_EPS))
    yc_n = jnp.concatenate(parts, axis=1) * cn_ref[...]
    merged = None
    for n, y in enumerate((ya_ref[...], yb_ref[...], yc_n)):
        z = z_ref[:, n * BRANCH_WIDTH:(n + 1) * BRANCH_WIDTH]
        br = (y * _silu(z)).astype(BF16)
        proj = jnp.dot(br, wbr_ref[n], preferred_element_type=F32)
        term = jax.nn.sigmoid(gate_ref[:, n * D_MODEL:(n + 1) * D_MODEL]) * proj
        merged = term if merged is None else merged + term
    out = x_ref[...] + jnp.dot(merged.astype(BF16), wout_ref[...], preferred_element_type=F32)
    if final:
        ms = jnp.mean(out * out, axis=-1, keepdims=True)
        out = out * lax.rsqrt(ms + NORM_EPS) * fg_ref[...]
    o_ref[...] = out


def _merge(x2d, ya, yb, yc, f32_out, cn, wbr, wout, fg, final, tm=256):
    t, d = x2d.shape
    kern = functools.partial(_merge_kernel, final=final)
    return pl.pallas_call(
        kern,
        grid=(t // tm,),
        in_specs=[
            pl.BlockSpec((tm, d), lambda i: (i, 0)),
            pl.BlockSpec((tm, 512), lambda i: (i, 0)),
            pl.BlockSpec((tm, 512), lambda i: (i, 0)),
            pl.BlockSpec((tm, 512), lambda i: (i, 0)),
            pl.BlockSpec((tm, 3 * D_MODEL), lambda i: (i, 0)),
            pl.BlockSpec((tm, 3 * BRANCH_WIDTH), lambda i: (i, 2)),
            pl.BlockSpec((1, 512), lambda i: (0, 0)),
            pl.BlockSpec((N_BRANCH, BRANCH_WIDTH, d), lambda i: (0, 0, 0)),
            pl.BlockSpec((d, d), lambda i: (0, 0)),
            pl.BlockSpec((1, d), lambda i: (0, 0)),
        ],
        out_specs=pl.BlockSpec((tm, d), lambda i: (i, 0)),
        out_shape=jax.ShapeDtypeStruct((t, d), F32),
        compiler_params=pltpu.CompilerParams(
            dimension_semantics=("arbitrary",), vmem_limit_bytes=VMEM_LIMIT),
        name="merge",
    )(x2d, ya, yb, yc, f32_out, f32_out, cn, wbr, wout, fg)


def _layer(x2d, batch, seq, ctab, stab, norm_g, w_in, b_in, f_bias, conv_w, conv_b, rel_bias,
           c_norm_w, w_branch, w_out, final_g, final):
    g = norm_g[None, :]
    w_rope = _pad_cols(w_in[:, _ROPE_COLS], ROPE_W).astype(BF16)
    w_ropep = _pad_cols(w_in[:, _ROPE_PARTNER], ROPE_W).astype(BF16)
    b_rope = _pad_cols(b_in[_ROPE_COLS], ROPE_W)[None, :]
    b_ropep = _pad_cols(b_in[_ROPE_PARTNER], ROPE_W)[None, :]
    rope_out = _proj_rope(x2d, g, w_rope, w_ropep, b_rope, b_ropep, ctab, stab, seq)
    bf_out, av_t = _proj_bf(x2d, g, w_in[:, _BF_COLS].astype(BF16), b_in[_BF_COLS][None, :])
    f32_out = _proj(x2d, g, _pad_cols(w_in[:, _F32_COLS], F32_W).astype(BF16),
                    _pad_cols(b_in[_F32_COLS], F32_W)[None, :], F32, F32_TN)

    y_a = _dsa(rope_out, av_t, f32_out, batch, seq)
    y_b = _band(bf_out, _band_bias_table(rel_bias), batch, seq)
    fbias_row = jnp.zeros((1, LANES), F32).at[0, _FG_LANE:_FG_LANE + C_HEADS].set(f_bias)
    y_c = _mlstm(f32_out, bf_out, fbias_row, conv_w, conv_b[None, :], batch, seq)
    return _merge(x2d, y_a, y_b, y_c, f32_out, c_norm_w[None, :], w_branch.astype(BF16),
                  w_out.astype(BF16), final_g[None, :], final)


def kernel(x, norm_g, w_in, b_in, f_bias, conv_w, conv_b, rel_bias, c_norm_w, w_branch, w_out, final_g):
    batch, seq, d = x.shape
    ctab, stab = _rope_tables(seq)
    x2d = x.reshape(batch * seq, d)
    depth = norm_g.shape[0]
    for l in range(depth):
        x2d = _layer(x2d, batch, seq, ctab, stab, norm_g[l], w_in[l], b_in[l], f_bias[l], conv_w[l],
                     conv_b[l], rel_bias[l], c_norm_w[l], w_branch[l], w_out[l], final_g, l == depth - 1)
    return x2d.reshape(batch, seq, d)
```

```python
import functools

import numpy as np
import jax
import jax.numpy as jnp
from jax import lax
from jax.experimental import pallas as pl
from jax.experimental.pallas import tpu as pltpu

F32 = jnp.float32
BF16 = jnp.bfloat16

D_MODEL = 1024
DEPTH = 2
CHUNK = 64
CHUNK_SHIFT = 6
NORM_EPS = 1e-6
ROPE_THETA = 500000.0

A_HEADS = 8
A_KV_HEADS = 2
A_HEAD_DIM = 64
IDX_HEADS = 4
IDX_DIM = 64
TOPK_MAX = 256
ROT_DIM = A_HEAD_DIM // 4
B_HEADS = 8
B_HEAD_DIM = 64
B_LEFT_CHUNKS = 8
B_MAX_REL = 128
C_HEADS = 4
C_HEAD_DIM = 128
C_CONV = 4
N_BRANCH = 3
BRANCH_WIDTH = 512

IN_NAMES = ('a_q', 'a_k', 'a_v', 'i_q', 'i_k', 'i_w', 'a_z', 'b_q', 'b_k', 'b_v', 'b_z',
            'c_q', 'c_k', 'c_v', 'c_i', 'c_f', 'c_o', 'c_z', 'gates')
IN_SPLITS = (512, 128, 128, 256, 64, 4, 512, 512, 512, 512, 512,
             512, 512, 512, 4, 4, 512, 512, 3 * D_MODEL)
_OFF = dict(zip(IN_NAMES, np.concatenate([[0], np.cumsum(IN_SPLITS)[:-1]]).tolist()))
_WID = dict(zip(IN_NAMES, IN_SPLITS))

LANES = 128
SUBLANES = 8
NEG_BIG = -1e30
VMEM_LIMIT = 56 * 1024 * 1024


_ROPE_NAMES = ('a_q', 'i_q', 'a_k', 'i_k')
ROPE_W = 1024
_BF_NAMES = ('b_q', 'b_k', 'b_v', 'c_v', 'a_v')
BF_W = 2176
_AV_OFF = 2048
_F32_NAMES = ('gates', 'a_z', 'b_z', 'c_z', 'c_o', 'c_q', 'c_k', 'i_w', 'c_i', 'c_f')
F32_W = 6272
F32_TN = 896


def _take_cols(a, names, width):
    parts = [a[..., _OFF[n]:_OFF[n] + _WID[n]] for n in names]
    used = sum(_WID[n] for n in names)
    if width > used:
        parts.append(jnp.zeros(a.shape[:-1] + (width - used,), a.dtype))
    return jnp.concatenate(parts, axis=-1)


def _rope_partner(a):
    half = ROT_DIM // 2
    heads = a.reshape(a.shape[:-1] + (a.shape[-1] // A_HEAD_DIM, A_HEAD_DIM))
    swapped = jnp.concatenate([heads[..., half:ROT_DIM], heads[..., :half], heads[..., ROT_DIM:]], axis=-1)
    return swapped.reshape(a.shape)


def _normed_bf16(x_ref, g_ref):
    x = x_ref[...]
    ms = jnp.mean(x * x, axis=-1, keepdims=True)
    return (x * lax.rsqrt(ms + NORM_EPS) * g_ref[...]).astype(BF16)


def _proj_kernel(x_ref, g_ref, w_ref, b_ref, o_ref, h_scr):
    @pl.when(pl.program_id(1) == 0)
    def _():
        h_scr[...] = _normed_bf16(x_ref, g_ref)

    acc = jnp.dot(h_scr[...], w_ref[...], preferred_element_type=F32) + b_ref[...]
    o_ref[...] = acc.astype(o_ref.dtype)


def _proj_bf_kernel(x_ref, g_ref, w_ref, b_ref, o_ref, vt_ref):
    h = _normed_bf16(x_ref, g_ref)
    acc = jnp.dot(h, w_ref[...], preferred_element_type=F32) + b_ref[...]
    o_ref[...] = acc.astype(o_ref.dtype)
    for r in range(acc.shape[0] // LANES):
        blk = acc[r * LANES:(r + 1) * LANES, _AV_OFF:_AV_OFF + LANES]
        vt_ref[:, r * LANES:(r + 1) * LANES] = blk.T.astype(vt_ref.dtype)


def _proj_rope_kernel(x_ref, g_ref, w_ref, wp_ref, b_ref, bp_ref, c_ref, s_ref, o_ref):
    h = _normed_bf16(x_ref, g_ref)
    acc = jnp.dot(h, w_ref[...], preferred_element_type=F32) + b_ref[...]
    accp = jnp.dot(h, wp_ref[...], preferred_element_type=F32) + bp_ref[...]
    reps = acc.shape[1] // LANES
    c = jnp.concatenate([c_ref[...]] * reps, axis=1)
    s = jnp.concatenate([s_ref[...]] * reps, axis=1)
    o_ref[...] = (acc * c + accp * s).astype(o_ref.dtype)


def _proj(x2d, g, w, b, out_dtype, tn, tm=512):
    t, d = x2d.shape
    n = w.shape[1]
    return pl.pallas_call(
        _proj_kernel,
        grid=(t // tm, n // tn),
        in_specs=[
            pl.BlockSpec((tm, d), lambda i, j: (i, 0)),
            pl.BlockSpec((1, d), lambda i, j: (0, 0)),
            pl.BlockSpec((d, tn), lambda i, j: (0, j)),
            pl.BlockSpec((1, tn), lambda i, j: (0, j)),
        ],
        out_specs=pl.BlockSpec((tm, tn), lambda i, j: (i, j)),
        out_shape=jax.ShapeDtypeStruct((t, n), out_dtype),
        scratch_shapes=[pltpu.VMEM((tm, d), BF16)],
        compiler_params=pltpu.CompilerParams(
            dimension_semantics=("arbitrary", "arbitrary"), vmem_limit_bytes=VMEM_LIMIT),
        name="proj",
    )(x2d, g, w, b)


def _proj_bf(x2d, g, w, b, tm=512):
    t, d = x2d.shape
    n = w.shape[1]
    return pl.pallas_call(
        _proj_bf_kernel,
        grid=(t // tm,),
        in_specs=[
            pl.BlockSpec((tm, d), lambda i: (i, 0)),
            pl.BlockSpec((1, d), lambda i: (0, 0)),
            pl.BlockSpec((d, n), lambda i: (0, 0)),
            pl.BlockSpec((1, n), lambda i: (0, 0)),
        ],
        out_specs=[pl.BlockSpec((tm, n), lambda i: (i, 0)),
                   pl.BlockSpec((LANES, tm), lambda i: (0, i))],
        out_shape=[jax.ShapeDtypeStruct((t, n), BF16), jax.ShapeDtypeStruct((LANES, t), BF16)],
        compiler_params=pltpu.CompilerParams(
            dimension_semantics=("arbitrary",), vmem_limit_bytes=VMEM_LIMIT),
        name="proj_bf",
    )(x2d, g, w, b)


def _proj_rope(x2d, g, w, wp, b, bp, ctab, stab, seq, tm=512):
    t, d = x2d.shape
    n = w.shape[1]
    nsb = seq // tm
    return pl.pallas_call(
        _proj_rope_kernel,
        grid=(t // tm,),
        in_specs=[
            pl.BlockSpec((tm, d), lambda i: (i, 0)),
            pl.BlockSpec((1, d), lambda i: (0, 0)),
            pl.BlockSpec((d, n), lambda i: (0, 0)),
            pl.BlockSpec((d, n), lambda i: (0, 0)),
            pl.BlockSpec((1, n), lambda i: (0, 0)),
            pl.BlockSpec((1, n), lambda i: (0, 0)),
            pl.BlockSpec((tm, LANES), lambda i: (i % nsb, 0)),
            pl.BlockSpec((tm, LANES), lambda i: (i % nsb, 0)),
        ],
        out_specs=pl.BlockSpec((tm, n), lambda i: (i, 0)),
        out_shape=jax.ShapeDtypeStruct((t, n), BF16),
        compiler_params=pltpu.CompilerParams(
            dimension_semantics=("arbitrary",), vmem_limit_bytes=VMEM_LIMIT),
        name="proj_rope",
    )(x2d, g, w, wp, b, bp, ctab, stab)


def _rope_tables(seq):
    half = ROT_DIM // 2
    inv_freq = 1.0 / (ROPE_THETA ** (jnp.arange(half, dtype=F32) * 2.0 / ROT_DIM))
    ang = jnp.arange(seq, dtype=F32)[:, None] * inv_freq[None, :]
    cos, sin = jnp.cos(ang), jnp.sin(ang)
    ones = jnp.ones((seq, A_HEAD_DIM - ROT_DIM), F32)
    zeros = jnp.zeros((seq, A_HEAD_DIM - ROT_DIM), F32)
    c64 = jnp.concatenate([cos, cos, ones], axis=1)
    s64 = jnp.concatenate([-sin, sin, zeros], axis=1)
    return jnp.tile(c64, (1, LANES // A_HEAD_DIM)), jnp.tile(s64, (1, LANES // A_HEAD_DIM))


DSA_QB = 128
DSA_KT = 512
_KEY_NEG_MAX = -2139095040
_GSZ = A_HEADS // A_KV_HEADS


def _key_to_f32(t):
    bits = jnp.where(t >= 0, t, t ^ jnp.int32(0x7FFFFFFF))
    return pltpu.bitcast(bits, F32)


def _dsa_kernel(q_ref, iq_ref, iw_ref, k_ref, ik_ref, vt_ref, o_ref,
                sc_scr, iqt_scr, wrow_scr, wq_scr, ltri_scr, sa_scr, sb_scr, m_scr, l_scr, acc_scr,
                *, n_sel):
    qb, kt = DSA_QB, DSA_KT
    nq4 = _GSZ * qb
    i = pl.program_id(1)
    n_kt = ((i + 1) * qb + kt - 1) // kt
    lane = lax.broadcasted_iota(jnp.int32, (qb, LANES), 1)
    low_half = lane < A_HEAD_DIM

    for h in range(IDX_HEADS):
        slab = iq_ref[:, (h // 2) * LANES:(h // 2 + 1) * LANES].astype(F32)
        if h % 2 == 1:
            slab = pltpu.roll(slab, A_HEAD_DIM, 1)
        iqt_scr[:, h * qb:(h + 1) * qb] = jnp.where(low_half, slab, 0.0).T.astype(BF16)
    iw_t = iw_ref[...].T
    for h in range(IDX_HEADS):
        wrow_scr[0:1, h * qb:(h + 1) * qb] = iw_t[h:h + 1, :] * (IDX_DIM ** -0.5 * IDX_HEADS ** -0.5)
    eye = jnp.where(lax.broadcasted_iota(jnp.int32, (qb, LANES), 0) == lane, 1.0, 0.0).astype(BF16)
    for h in range(A_HEADS):
        g, hh = h // _GSZ, h % _GSZ
        slab = q_ref[:, (h // 2) * LANES:(h // 2 + 1) * LANES].astype(F32)
        if (h % 2) != g:
            slab = pltpu.roll(slab, A_HEAD_DIM, 1)
        keep = low_half if g == 0 else jnp.logical_not(low_half)
        slab = jnp.where(keep, slab, 0.0) * (A_HEAD_DIM ** -0.5)
        wq_scr[g, 0:LANES, hh * qb:(hh + 1) * qb] = slab.T.astype(BF16)
        wq_scr[g, LANES:2 * LANES, hh * qb:(hh + 1) * qb] = eye
    tri_r = lax.broadcasted_iota(jnp.int32, (kt, kt), 0)
    tri_c = lax.broadcasted_iota(jnp.int32, (kt, kt), 1)
    ltri_scr[...] = jnp.where(tri_c < tri_r, 1.0, 0.0).astype(BF16)

    q_chunk = jnp.right_shift(i * qb + lax.broadcasted_iota(jnp.int32, (kt, qb), 1), CHUNK_SHIFT)

    def score_tile(t, masked):
        off = pl.multiple_of(t * kt, kt)
        lg = jnp.dot(ik_ref[pl.ds(off, kt), :], iqt_scr[...], preferred_element_type=F32)
        lg = jnp.maximum(lg, 0.0) * wrow_scr[0:1, :]
        sc = (lg[:, 0:qb] + lg[:, qb:2 * qb]) + (lg[:, 2 * qb:3 * qb] + lg[:, 3 * qb:4 * qb])
        if masked:
            k_chunk = jnp.right_shift(off + lax.broadcasted_iota(jnp.int32, (kt, qb), 0), CHUNK_SHIFT)
            sc = jnp.where(k_chunk <= q_chunk, sc, -jnp.inf)
        sc_scr[pl.ds(off, kt), :] = sc

    def p1_body(t, c):
        score_tile(t, False)
        return c

    lax.fori_loop(0, n_kt - 1, p1_body, 0)
    score_tile(n_kt - 1, True)

    n_acc = 4
    zero_acc = tuple(jnp.zeros((SUBLANES, qb), jnp.int32) for _ in range(n_acc))

    def count_rows(hit_fn):
        def body(t, accs):
            off = pl.multiple_of(t * kt, kt)
            accs = list(accs)
            for r in range(kt // SUBLANES):
                s = sc_scr[pl.ds(off + r * SUBLANES, SUBLANES), :]
                accs[r % n_acc] = accs[r % n_acc] + jnp.where(hit_fn(s), 1, 0)
            return tuple(accs)
        accs = lax.fori_loop(0, n_kt, body, zero_acc)
        tot = (accs[0] + accs[1]) + (accs[2] + accs[3])
        return jnp.sum(tot.astype(F32), axis=0, keepdims=True)

    def bit_body(it, t):
        bit = lax.shift_left(jnp.int32(1), 31 - it)
        cand = t ^ bit
        cand_f = _key_to_f32(cand)
        cnt = count_rows(lambda s: s >= cand_f)
        return jnp.where(cnt >= n_sel, cand, t)

    t_key = lax.fori_loop(0, 32, bit_body, jnp.full((1, qb), -2 ** 31, jnp.int32))
    t_key = jnp.maximum(t_key, _KEY_NEG_MAX)
    tau = _key_to_f32(t_key)
    c_gt = count_rows(lambda s: s > tau)
    need = n_sel - c_gt

    m_scr[...] = jnp.full(m_scr.shape, NEG_BIG, F32)
    l_scr[...] = jnp.zeros(l_scr.shape, F32)
    acc_scr[...] = jnp.zeros(acc_scr.shape, F32)

    def tile_offset(t):
        return pl.multiple_of(jnp.minimum(t, n_kt - 1) * kt, kt)

    def tie_ranks(t, ties_before):
        sc = sc_scr[pl.ds(tile_offset(t), kt), :]
        tie_f = jnp.where(sc == tau, 1.0, 0.0)
        rank = jnp.dot(ltri_scr[...], tie_f.astype(BF16), preferred_element_type=F32) + ties_before
        return sc, tie_f, rank

    def masked_keys(t, sc, tie_f, rank):
        sel = (sc > tau) | ((tie_f > 0.0) & (rank < need))
        open_bias = jnp.where(t < n_kt, 0.0, NEG_BIG)
        bias = jnp.where(sel, open_bias, NEG_BIG).astype(BF16)
        return jnp.concatenate([k_ref[pl.ds(tile_offset(t), kt), :], bias], axis=1)

    def accumulate(t, src, g):
        m_prev = m_scr[g]
        m_next = jnp.maximum(m_prev, jnp.max(src[g], axis=0, keepdims=True))
        alpha = jnp.exp(m_prev - m_next)
        p = jnp.exp(src[g] - m_next)
        l_scr[g] = alpha * l_scr[g] + jnp.sum(p, axis=0, keepdims=True)
        m_scr[g] = m_next
        vt_g = vt_ref[g * A_HEAD_DIM:(g + 1) * A_HEAD_DIM, pl.ds(tile_offset(t), kt)]
        pv = jnp.dot(vt_g, p.astype(BF16), preferred_element_type=F32)
        acc_scr[g] = alpha * acc_scr[g] + pv

    def step(t, ties, src, dst):
        sc, tie_f, rank = tie_ranks(t + 1, ties)
        accumulate(t, src, 0)
        lhs = masked_keys(t + 1, sc, tie_f, rank)
        dst[0] = jnp.dot(lhs, wq_scr[0], preferred_element_type=F32)
        accumulate(t, src, 1)
        dst[1] = jnp.dot(lhs, wq_scr[1], preferred_element_type=F32)
        return rank[kt - 1:kt, :] + tie_f[kt - 1:kt, :]

    def tile_pair(u, ties):
        ties = step(2 * u, ties, sa_scr, sb_scr)
        return step(2 * u + 1, ties, sb_scr, sa_scr)

    zero_ties = jnp.zeros((1, qb), F32)
    sc0, tie0, rank0 = tie_ranks(0, zero_ties)
    lhs0 = masked_keys(0, sc0, tie0, rank0)
    for g in range(A_KV_HEADS):
        sa_scr[g] = jnp.dot(lhs0, wq_scr[g], preferred_element_type=F32)
    lax.fori_loop(0, (n_kt + 1) // 2, tile_pair, rank0[kt - 1:kt, :] + tie0[kt - 1:kt, :])
    for j in range(A_HEADS // 2):
        g, hh0 = j // 2, 2 * (j % 2)
        o_t = acc_scr[g] / l_scr[g]
        pair = jnp.concatenate([o_t[:, hh0 * qb:(hh0 + 1) * qb],
                                o_t[:, (hh0 + 1) * qb:(hh0 + 2) * qb]], axis=0)
        o_ref[:, j * LANES:(j + 1) * LANES] = pair.T


def _dsa(rope_out, av_t, f32_out, batch, seq):
    qb = DSA_QB
    nqb = seq // qb
    n_sel = min(TOPK_MAX, seq // 4)
    nq4 = _GSZ * qb
    kern = functools.partial(_dsa_kernel, n_sel=n_sel)
    return pl.pallas_call(
        kern,
        grid=(batch, nqb),
        in_specs=[
            pl.BlockSpec((qb, 512), lambda b, i: (b * nqb + i, 0)),
            pl.BlockSpec((qb, 256), lambda b, i: (b * nqb + i, 2)),
            pl.BlockSpec((qb, LANES), lambda b, i: (b * nqb + i, F32_W // LANES - 1)),
            pl.BlockSpec((seq, LANES), lambda b, i: (b, 6)),
            pl.BlockSpec((seq, LANES), lambda b, i: (b, 7)),
            pl.BlockSpec((LANES, seq), lambda b, i: (0, b)),
        ],
        out_specs=pl.BlockSpec((qb, 512), lambda b, i: (b * nqb + i, 0)),
        out_shape=jax.ShapeDtypeStruct((batch * seq, 512), F32),
        scratch_shapes=[
            pltpu.VMEM((seq, qb), F32),
            pltpu.VMEM((LANES, IDX_HEADS * qb), BF16),
            pltpu.VMEM((SUBLANES, IDX_HEADS * qb), F32),
            pltpu.VMEM((A_KV_HEADS, 2 * LANES, nq4), BF16),
            pltpu.VMEM((DSA_KT, DSA_KT), BF16),
            pltpu.VMEM((A_KV_HEADS, DSA_KT, nq4), F32),
            pltpu.VMEM((A_KV_HEADS, DSA_KT, nq4), F32),
            pltpu.VMEM((A_KV_HEADS, 1, nq4), F32),
            pltpu.VMEM((A_KV_HEADS, 1, nq4), F32),
            pltpu.VMEM((A_KV_HEADS, A_HEAD_DIM, nq4), F32),
        ],
        compiler_params=pltpu.CompilerParams(
            dimension_semantics=("arbitrary", "arbitrary"), vmem_limit_bytes=VMEM_LIMIT),
        name="dsa",
    )(rope_out, rope_out, f32_out, rope_out, rope_out, av_t)


BAND_QB = 128
BAND_KB = 5
BAND_KW = BAND_QB * BAND_KB


def _band_bias_table(rel_bias):
    r = np.arange(BAND_QB)[:, None]
    c = np.arange(BAND_KW)[None, :]
    a, j = r // CHUNK, c // CHUNK
    visible = (j >= a) & (j <= a + B_LEFT_CHUNKS)
    rel = r - c + B_LEFT_CHUNKS * CHUNK
    idx = np.clip(rel, -B_MAX_REL, B_MAX_REL) + B_MAX_REL
    tab = rel_bias.astype(F32)[:, idx]
    return jnp.where(jnp.asarray(visible)[None], tab, NEG_BIG)


def _band_kernel(q_ref, k0, k1, k2, k3, k4, v0, v1, v2, v3, v4, bias_ref, o_ref, k_scr, v_scr):
    qb, kw = BAND_QB, BAND_KW
    i = pl.program_id(1)
    for j, (kr, vr) in enumerate(((k0, v0), (k1, v1), (k2, v2), (k3, v3), (k4, v4))):
        k_scr[j * qb:(j + 1) * qb, :] = kr[...]
        v_scr[j * qb:(j + 1) * qb, :] = vr[...]
    key_pos = (i - (BAND_KB - 1)) * qb + lax.broadcasted_iota(jnp.int32, (qb, kw), 1)
    in_seq = key_pos >= 0
    lane = lax.broadcasted_iota(jnp.int32, (qb, LANES), 1)
    low_half = lane < B_HEAD_DIM
    scale = B_HEAD_DIM ** -0.5
    for j in range(B_HEADS // 2):
        sl = slice(j * LANES, (j + 1) * LANES)
        q2 = q_ref[:, sl].astype(F32)
        k2s = k_scr[:, sl]
        v2s = v_scr[:, sl]
        halves = []
        for par in range(2):
            keep = low_half if par == 0 else jnp.logical_not(low_half)
            qh = jnp.where(keep, q2, 0.0).astype(BF16)
            s = lax.dot_general(qh, k2s, (((1,), (1,)), ((), ())), preferred_element_type=F32)
            s = s * scale + bias_ref[2 * j + par]
            s = jnp.where(in_seq, s, NEG_BIG)
            m = jnp.max(s, axis=1, keepdims=True)
            p = jnp.exp(s - m)
            p = p / jnp.sum(p, axis=1, keepdims=True)
            halves.append(jnp.dot(p.astype(BF16), v2s, preferred_element_type=F32))
        o_ref[:, sl] = jnp.where(low_half, halves[0], halves[1])


def _band(bf_out, bias_tab, batch, seq):
    qb = BAND_QB
    nqb = seq // qb

    def kv_spec(col, j):
        return pl.BlockSpec(
            (qb, 512), lambda b, i: (b * nqb + jnp.maximum(i - (BAND_KB - 1) + j, 0), col))

    return pl.pallas_call(
        _band_kernel,
        grid=(batch, nqb),
        in_specs=([pl.BlockSpec((qb, 512), lambda b, i: (b * nqb + i, 0))]
                  + [kv_spec(1, j) for j in range(BAND_KB)]
                  + [kv_spec(2, j) for j in range(BAND_KB)]
                  + [pl.BlockSpec((B_HEADS, qb, BAND_KW), lambda b, i: (0, 0, 0))]),
        out_specs=pl.BlockSpec((qb, 512), lambda b, i: (b * nqb + i, 0)),
        out_shape=jax.ShapeDtypeStruct((batch * seq, 512), F32),
        scratch_shapes=[pltpu.VMEM((BAND_KW, 512), BF16), pltpu.VMEM((BAND_KW, 512), BF16)],
        compiler_params=pltpu.CompilerParams(
            dimension_semantics=("arbitrary", "arbitrary"), vmem_limit_bytes=VMEM_LIMIT),
        name="band",
    )(bf_out, *([bf_out] * (2 * BAND_KB)), bias_tab)


_IG_LANE = IDX_HEADS
_FG_LANE = IDX_HEADS + C_HEADS


def _log_sigmoid(x):
    return jnp.minimum(x, 0.0) - jnp.log1p(jnp.exp(-jnp.abs(x)))


def _mlstm_kernel(qk_ref, v_ref, o_ref, sm_ref, fb_ref, cw_ref, cb_ref, out_ref,
                  xx_scr, c_scr, n_scr, m_scr, tr_scr):
    L, d = CHUNK, C_HEAD_DIM
    c_idx = pl.program_id(1)

    @pl.when(c_idx == 0)
    def _():
        xx_scr[0:8, :] = jnp.zeros((8, xx_scr.shape[1]), F32)
        c_scr[...] = jnp.zeros(c_scr.shape, F32)
        n_scr[...] = jnp.zeros(n_scr.shape, F32)
        m_scr[...] = jnp.zeros(m_scr.shape, F32)

    x = qk_ref[...]
    xx_scr[8:8 + L, :] = x
    y = cb_ref[...]
    for j in range(C_CONV):
        y = y + cw_ref[j:j + 1, :] * xx_scr[8 - (C_CONV - 1) + j:8 - (C_CONV - 1) + j + L, :]
    xx_scr[0:8, :] = x[L - 8:L, :]
    qk = y * jax.nn.sigmoid(y)

    a = sm_ref[...] + fb_ref[...]
    logf = _log_sigmoid(a)
    row = lax.broadcasted_iota(jnp.int32, (L, L), 0)
    col = lax.broadcasted_iota(jnp.int32, (L, L), 1)
    causal = col <= row
    tri = jnp.where(causal, 1.0, 0.0).astype(F32)
    bcol = jnp.dot(tri, logf, preferred_element_type=F32, precision=lax.Precision.HIGHEST)
    lane = lax.broadcasted_iota(jnp.int32, (L, LANES), 1)
    mix = jnp.where(lane >= _FG_LANE, bcol, a)
    tr_scr[0:L, :] = mix
    tr_scr[L:2 * L, :] = jnp.zeros((L, LANES), F32)
    rows_t = jnp.transpose(tr_scr[...])

    for h in range(C_HEADS):
        sl = slice(h * d, (h + 1) * d)
        q = qk[:, sl] * (d ** -0.5)
        k = qk[:, C_HEADS * d + h * d:C_HEADS * d + (h + 1) * d]
        v = v_ref[:, sl]
        qb16 = q.astype(BF16)
        b_col = bcol[:, _FG_LANE + h:_FG_LANE + h + 1]
        i_col = a[:, _IG_LANE + h:_IG_LANE + h + 1]
        b_row = rows_t[_FG_LANE + h:_FG_LANE + h + 1, 0:L]
        i_row = rows_t[_IG_LANE + h:_IG_LANE + h + 1, 0:L]
        g = b_col[L - 1:L, :]
        m_prev = m_scr[h][:, 0:1]
        c_prev = c_scr[h]
        n_prev = n_scr[h]

        dmat = jnp.where(causal, b_col - b_row + i_row, NEG_BIG)
        m_inter = b_col + m_prev
        m_row = jnp.maximum(m_inter, jnp.max(dmat, axis=1, keepdims=True))
        inter = jnp.exp(m_inter - m_row)
        s_qk = lax.dot_general(qb16, k.astype(BF16), (((1,), (1,)), ((), ())), preferred_element_type=F32)
        s_qk = s_qk * jnp.exp(dmat - m_row)
        num = (jnp.dot(s_qk.astype(BF16), v, preferred_element_type=F32)
               + inter * jnp.dot(qb16, c_prev.astype(BF16), preferred_element_type=F32))
        den = jnp.sum(s_qk, axis=1, keepdims=True) + inter * jnp.sum(q * n_prev, axis=1, keepdims=True)
        hval = num / jnp.maximum(jnp.abs(den), jnp.exp(-m_row))
        out_ref[:, sl] = jax.nn.sigmoid(o_ref[:, sl]) * hval

        w_log = g - b_col + i_col
        m_loc = jnp.max(w_log, axis=0, keepdims=True)
        kw = k * jnp.exp(w_log - m_loc)
        kv = lax.dot_general(kw.astype(BF16), v, (((0,), (0,)), ((), ())), preferred_element_type=F32)
        ksum = jnp.sum(kw, axis=0, keepdims=True)
        m_new = jnp.maximum(g + m_prev, m_loc)
        fa = jnp.exp(g + m_prev - m_new)
        fb = jnp.exp(m_loc - m_new)
        c_scr[h] = fa * c_prev + fb * kv
        n_scr[h] = fa * n_prev + fb * ksum
        m_scr[h] = jnp.broadcast_to(m_new, (1, LANES))


def _mlstm(f32_out, bf_out, fbias_row, conv_w, conv_b, batch, seq):
    L = CHUNK
    nc = seq // L
    return pl.pallas_call(
        _mlstm_kernel,
        grid=(batch, nc),
        in_specs=[
            pl.BlockSpec((L, 1024), lambda b, c: (b * nc + c, 5)),
            pl.BlockSpec((L, 512), lambda b, c: (b * nc + c, 3)),
            pl.BlockSpec((L, 512), lambda b, c: (b * nc + c, 9)),
            pl.BlockSpec((L, LANES), lambda b, c: (b * nc + c, F32_W // LANES - 1)),
            pl.BlockSpec((1, LANES), lambda b, c: (0, 0)),
            pl.BlockSpec((C_CONV, 1024), lambda b, c: (0, 0)),
            pl.BlockSpec((1, 1024), lambda b, c: (0, 0)),
        ],
        out_specs=pl.BlockSpec((L, 512), lambda b, c: (b * nc + c, 0)),
        out_shape=jax.ShapeDtypeStruct((batch * seq, 512), F32),
        scratch_shapes=[
            pltpu.VMEM((8 + L, 1024), F32),
            pltpu.VMEM((C_HEADS, C_HEAD_DIM, C_HEAD_DIM), F32),
            pltpu.VMEM((C_HEADS, 1, C_HEAD_DIM), F32),
            pltpu.VMEM((C_HEADS, 1, LANES), F32),
            pltpu.VMEM((2 * L, LANES), F32),
        ],
        compiler_params=pltpu.CompilerParams(
            dimension_semantics=("arbitrary", "arbitrary"), vmem_limit_bytes=VMEM_LIMIT),
        name="mlstm",
    )(f32_out, bf_out, f32_out, f32_out, fbias_row, conv_w, conv_b)


def _silu(x):
    return x * jax.nn.sigmoid(x)


def _merge_kernel(x_ref, ya_ref, yb_ref, yc_ref, gate_ref, z_ref, cn_ref, wbr_ref, wout_ref, fg_ref,
                  o_ref, *, final):
    d = C_HEAD_DIM
    yc = yc_ref[...]
    parts = []
    for h in range(C_HEADS):
        hv = yc[:, h * d:(h + 1) * d]
        hc = hv - jnp.mean(hv, axis=-1, keepdims=True)
        parts.append(hc * lax.rsqrt(jnp.mean(hc * hc, axis=-1, keepdims=True) + NORM_EPS))
    yc_n = jnp.concatenate(parts, axis=1) * cn_ref[...]
    merged = None
    for n, y in enumerate((ya_ref[...], yb_ref[...], yc_n)):
        z = z_ref[:, n * BRANCH_WIDTH:(n + 1) * BRANCH_WIDTH]
        br = (y * _silu(z)).astype(BF16)
        proj = jnp.dot(br, wbr_ref[n], preferred_element_type=F32)
        term = jax.nn.sigmoid(gate_ref[:, n * D_MODEL:(n + 1) * D_MODEL]) * proj
        merged = term if merged is None else merged + term
    out = x_ref[...] + jnp.dot(merged.astype(BF16), wout_ref[...], preferred_element_type=F32)
    if final:
        ms = jnp.mean(out * out, axis=-1, keepdims=True)
        out = out * lax.rsqrt(ms + NORM_EPS) * fg_ref[...]
    o_ref[...] = out


def _merge(x2d, ya, yb, yc, f32_out, cn, wbr, wout, fg, final, tm=256):
    t, d = x2d.shape
    kern = functools.partial(_merge_kernel, final=final)
    return pl.pallas_call(
        kern,
        grid=(t // tm,),
        in_specs=[
            pl.BlockSpec((tm, d), lambda i: (i, 0)),
            pl.BlockSpec((tm, 512), lambda i: (i, 0)),
            pl.BlockSpec((tm, 512), lambda i: (i, 0)),
            pl.BlockSpec((tm, 512), lambda i: (i, 0)),
            pl.BlockSpec((tm, 3 * D_MODEL), lambda i: (i, 0)),
            pl.BlockSpec((tm, 3 * BRANCH_WIDTH), lambda i: (i, 2)),
            pl.BlockSpec((1, 512), lambda i: (0, 0)),
            pl.BlockSpec((N_BRANCH, BRANCH_WIDTH, d), lambda i: (0, 0, 0)),
            pl.BlockSpec((d, d), lambda i: (0, 0)),
            pl.BlockSpec((1, d), lambda i: (0, 0)),
        ],
        out_specs=pl.BlockSpec((tm, d), lambda i: (i, 0)),
        out_shape=jax.ShapeDtypeStruct((t, d), F32),
        compiler_params=pltpu.CompilerParams(
            dimension_semantics=("arbitrary",), vmem_limit_bytes=VMEM_LIMIT),
        name="merge",
    )(x2d, ya, yb, yc, f32_out, f32_out, cn, wbr, wout, fg)


def _layer(x2d, batch, seq, ctab, stab, norm_g, w_in, b_in, f_bias, conv_w, conv_b, rel_bias,
           c_norm_w, w_branch, w_out, final_g, final):
    g = norm_g[None, :]
    b_row = b_in[None, :]
    w_rope = _take_cols(w_in, _ROPE_NAMES, ROPE_W)
    b_rope = _take_cols(b_row, _ROPE_NAMES, ROPE_W)
    rope_out = _proj_rope(x2d, g, w_rope.astype(BF16), _rope_partner(w_rope).astype(BF16),
                          b_rope, _rope_partner(b_rope), ctab, stab, seq)
    bf_out, av_t = _proj_bf(x2d, g, _take_cols(w_in, _BF_NAMES, BF_W).astype(BF16),
                            _take_cols(b_row, _BF_NAMES, BF_W))
    f32_out = _proj(x2d, g, _take_cols(w_in, _F32_NAMES, F32_W).astype(BF16),
                    _take_cols(b_row, _F32_NAMES, F32_W), F32, F32_TN, tm=min(1024, seq))

    y_a = _dsa(rope_out, av_t, f32_out, batch, seq)
    y_b = _band(bf_out, _band_bias_table(rel_bias), batch, seq)
    fbias_row = jnp.zeros((1, LANES), F32).at[0, _FG_LANE:_FG_LANE + C_HEADS].set(f_bias)
    y_c = _mlstm(f32_out, bf_out, fbias_row, conv_w, conv_b[None, :], batch, seq)
    return _merge(x2d, y_a, y_b, y_c, f32_out, c_norm_w[None, :], w_branch.astype(BF16),
                  w_out.astype(BF16), final_g[None, :], final)


def kernel(x, norm_g, w_in, b_in, f_bias, conv_w, conv_b, rel_bias, c_norm_w, w_branch, w_out, final_g):
    batch, seq, d = x.shape
    ctab, stab = _rope_tables(seq)
    x2d = x.reshape(batch * seq, d)
    depth = norm_g.shape[0]
    for l in range(depth):
        x2d = _layer(x2d, batch, seq, ctab, stab, norm_g[l], w_in[l], b_in[l], f_bias[l], conv_w[l],
                     conv_b[l], rel_bias[l], c_norm_w[l], w_branch[l], w_out[l], final_g, l == depth - 1)
    return x2d.reshape(batch, seq, d)
```

```python
import functools

import numpy as np
import jax
import jax.numpy as jnp
from jax import lax
from jax.experimental import pallas as pl
from jax.experimental.pallas import tpu as pltpu

F32 = jnp.float32
BF16 = jnp.bfloat16

D_MODEL = 1024
DEPTH = 2
CHUNK = 64
CHUNK_SHIFT = 6
NORM_EPS = 1e-6
ROPE_THETA = 500000.0

A_HEADS = 8
A_KV_HEADS = 2
A_HEAD_DIM = 64
IDX_HEADS = 4
IDX_DIM = 64
TOPK_MAX = 256
ROT_DIM = A_HEAD_DIM // 4
B_HEADS = 8
B_HEAD_DIM = 64
B_LEFT_CHUNKS = 8
B_MAX_REL = 128
C_HEADS = 4
C_HEAD_DIM = 128
C_CONV = 4
N_BRANCH = 3
BRANCH_WIDTH = 512

IN_NAMES = ('a_q', 'a_k', 'a_v', 'i_q', 'i_k', 'i_w', 'a_z', 'b_q', 'b_k', 'b_v', 'b_z',
            'c_q', 'c_k', 'c_v', 'c_i', 'c_f', 'c_o', 'c_z', 'gates')
IN_SPLITS = (512, 128, 128, 256, 64, 4, 512, 512, 512, 512, 512,
             512, 512, 512, 4, 4, 512, 512, 3 * D_MODEL)
_OFF = dict(zip(IN_NAMES, np.concatenate([[0], np.cumsum(IN_SPLITS)[:-1]]).tolist()))
_WID = dict(zip(IN_NAMES, IN_SPLITS))

LANES = 128
SUBLANES = 8
NEG_BIG = -1e30
VMEM_LIMIT = 56 * 1024 * 1024


_ROPE_NAMES = ('a_q', 'i_q', 'a_k', 'i_k')
ROPE_W = 1024
_BF_NAMES = ('b_q', 'b_k', 'b_v', 'c_v', 'a_v')
BF_W = 2176
_AV_OFF = 2048
_F32_NAMES = ('gates', 'a_z', 'b_z', 'c_z', 'c_o', 'c_q', 'c_k', 'i_w', 'c_i', 'c_f')
F32_W = 6272
F32_TN = 896


def _take_cols(a, names, width):
    idx = np.concatenate([np.arange(_OFF[n], _OFF[n] + _WID[n]) for n in names])
    out = a[..., idx]
    return jnp.pad(out, [(0, 0)] * (a.ndim - 1) + [(0, width - idx.shape[0])])


def _rope_partner(a):
    half = ROT_DIM // 2
    heads = a.reshape(a.shape[:-1] + (a.shape[-1] // A_HEAD_DIM, A_HEAD_DIM))
    swapped = jnp.concatenate([heads[..., half:ROT_DIM], heads[..., :half], heads[..., ROT_DIM:]], axis=-1)
    return swapped.reshape(a.shape)


def _normed_bf16(x_ref, g_ref):
    x = x_ref[...]
    ms = jnp.mean(x * x, axis=-1, keepdims=True)
    return (x * lax.rsqrt(ms + NORM_EPS) * g_ref[...]).astype(BF16)


def _proj_kernel(x_ref, g_ref, w_ref, b_ref, o_ref, h_scr):
    @pl.when(pl.program_id(1) == 0)
    def _():
        h_scr[...] = _normed_bf16(x_ref, g_ref)

    acc = jnp.dot(h_scr[...], w_ref[...], preferred_element_type=F32) + b_ref[...]
    o_ref[...] = acc.astype(o_ref.dtype)


def _proj_bf_kernel(x_ref, g_ref, w_ref, b_ref, o_ref, vt_ref):
    h = _normed_bf16(x_ref, g_ref)
    acc = jnp.dot(h, w_ref[...], preferred_element_type=F32) + b_ref[...]
    o_ref[...] = acc.astype(o_ref.dtype)
    for r in range(acc.shape[0] // LANES):
        blk = acc[r * LANES:(r + 1) * LANES, _AV_OFF:_AV_OFF + LANES]
        vt_ref[:, r * LANES:(r + 1) * LANES] = blk.T.astype(vt_ref.dtype)


def _proj_rope_kernel(x_ref, g_ref, w_ref, wp_ref, b_ref, bp_ref, c_ref, s_ref, o_ref):
    h = _normed_bf16(x_ref, g_ref)
    acc = jnp.dot(h, w_ref[...], preferred_element_type=F32) + b_ref[...]
    accp = jnp.dot(h, wp_ref[...], preferred_element_type=F32) + bp_ref[...]
    reps = acc.shape[1] // LANES
    c = jnp.concatenate([c_ref[...]] * reps, axis=1)
    s = jnp.concatenate([s_ref[...]] * reps, axis=1)
    o_ref[...] = (acc * c + accp * s).astype(o_ref.dtype)


def _proj(x2d, g, w, b, out_dtype, tn, tm=512):
    t, d = x2d.shape
    n = w.shape[1]
    return pl.pallas_call(
        _proj_kernel,
        grid=(t // tm, n // tn),
        in_specs=[
            pl.BlockSpec((tm, d), lambda i, j: (i, 0)),
            pl.BlockSpec((1, d), lambda i, j: (0, 0)),
            pl.BlockSpec((d, tn), lambda i, j: (0, j)),
            pl.BlockSpec((1, tn), lambda i, j: (0, j)),
        ],
        out_specs=pl.BlockSpec((tm, tn), lambda i, j: (i, j)),
        out_shape=jax.ShapeDtypeStruct((t, n), out_dtype),
        scratch_shapes=[pltpu.VMEM((tm, d), BF16)],
        compiler_params=pltpu.CompilerParams(
            dimension_semantics=("arbitrary", "arbitrary"), vmem_limit_bytes=VMEM_LIMIT),
        name="proj",
    )(x2d, g, w, b)


def _proj_bf(x2d, g, w, b, tm=512):
    t, d = x2d.shape
    n = w.shape[1]
    return pl.pallas_call(
        _proj_bf_kernel,
        grid=(t // tm,),
        in_specs=[
            pl.BlockSpec((tm, d), lambda i: (i, 0)),
            pl.BlockSpec((1, d), lambda i: (0, 0)),
            pl.BlockSpec((d, n), lambda i: (0, 0)),
            pl.BlockSpec((1, n), lambda i: (0, 0)),
        ],
        out_specs=[pl.BlockSpec((tm, n), lambda i: (i, 0)),
                   pl.BlockSpec((LANES, tm), lambda i: (0, i))],
        out_shape=[jax.ShapeDtypeStruct((t, n), BF16), jax.ShapeDtypeStruct((LANES, t), BF16)],
        compiler_params=pltpu.CompilerParams(
            dimension_semantics=("arbitrary",), vmem_limit_bytes=VMEM_LIMIT),
        name="proj_bf",
    )(x2d, g, w, b)


def _proj_rope(x2d, g, w, wp, b, bp, ctab, stab, seq, tm=512):
    t, d = x2d.shape
    n = w.shape[1]
    nsb = seq // tm
    return pl.pallas_call(
        _proj_rope_kernel,
        grid=(t // tm,),
        in_specs=[
            pl.BlockSpec((tm, d), lambda i: (i, 0)),
            pl.BlockSpec((1, d), lambda i: (0, 0)),
            pl.BlockSpec((d, n), lambda i: (0, 0)),
            pl.BlockSpec((d, n), lambda i: (0, 0)),
            pl.BlockSpec((1, n), lambda i: (0, 0)),
            pl.BlockSpec((1, n), lambda i: (0, 0)),
            pl.BlockSpec((tm, LANES), lambda i: (i % nsb, 0)),
            pl.BlockSpec((tm, LANES), lambda i: (i % nsb, 0)),
        ],
        out_specs=pl.BlockSpec((tm, n), lambda i: (i, 0)),
        out_shape=jax.ShapeDtypeStruct((t, n), BF16),
        compiler_params=pltpu.CompilerParams(
            dimension_semantics=("arbitrary",), vmem_limit_bytes=VMEM_LIMIT),
        name="proj_rope",
    )(x2d, g, w, wp, b, bp, ctab, stab)


def _rope_tables(seq):
    half = ROT_DIM // 2
    inv_freq = 1.0 / (ROPE_THETA ** (jnp.arange(half, dtype=F32) * 2.0 / ROT_DIM))
    ang = jnp.arange(seq, dtype=F32)[:, None] * inv_freq[None, :]
    cos, sin = jnp.cos(ang), jnp.sin(ang)
    ones = jnp.ones((seq, A_HEAD_DIM - ROT_DIM), F32)
    zeros = jnp.zeros((seq, A_HEAD_DIM - ROT_DIM), F32)
    c64 = jnp.concatenate([cos, cos, ones], axis=1)
    s64 = jnp.concatenate([-sin, sin, zeros], axis=1)
    return jnp.tile(c64, (1, LANES // A_HEAD_DIM)), jnp.tile(s64, (1, LANES // A_HEAD_DIM))


DSA_QB = 128
DSA_KT = 512
_KEY_NEG_MAX = -2139095040
_GSZ = A_HEADS // A_KV_HEADS


def _key_to_f32(t):
    bits = jnp.where(t >= 0, t, t ^ jnp.int32(0x7FFFFFFF))
    return pltpu.bitcast(bits, F32)


def _dsa_kernel(q_ref, iq_ref, iw_ref, k_ref, ik_ref, vt_ref, o_ref,
                sc_scr, iqt_scr, wrow_scr, wq_scr, ltri_scr, sa_scr, sb_scr, m_scr, l_scr, acc_scr,
                *, n_sel):
    qb, kt = DSA_QB, DSA_KT
    nq4 = _GSZ * qb
    i = pl.program_id(1)
    n_kt = ((i + 1) * qb + kt - 1) // kt
    lane = lax.broadcasted_iota(jnp.int32, (qb, LANES), 1)
    low_half = lane < A_HEAD_DIM

    for h in range(IDX_HEADS):
        slab = iq_ref[:, (h // 2) * LANES:(h // 2 + 1) * LANES].astype(F32)
        if h % 2 == 1:
            slab = pltpu.roll(slab, A_HEAD_DIM, 1)
        iqt_scr[:, h * qb:(h + 1) * qb] = jnp.where(low_half, slab, 0.0).T.astype(BF16)
    iw_t = iw_ref[...].T
    for h in range(IDX_HEADS):
        wrow_scr[0:1, h * qb:(h + 1) * qb] = iw_t[h:h + 1, :] * (IDX_DIM ** -0.5 * IDX_HEADS ** -0.5)
    eye = jnp.where(lax.broadcasted_iota(jnp.int32, (qb, LANES), 0) == lane, 1.0, 0.0).astype(BF16)
    for h in range(A_HEADS):
        g, hh = h // _GSZ, h % _GSZ
        slab = q_ref[:, (h // 2) * LANES:(h // 2 + 1) * LANES].astype(F32)
        if (h % 2) != g:
            slab = pltpu.roll(slab, A_HEAD_DIM, 1)
        keep = low_half if g == 0 else jnp.logical_not(low_half)
        slab = jnp.where(keep, slab, 0.0) * (A_HEAD_DIM ** -0.5)
        wq_scr[g, 0:LANES, hh * qb:(hh + 1) * qb] = slab.T.astype(BF16)
        wq_scr[g, LANES:2 * LANES, hh * qb:(hh + 1) * qb] = eye
    tri_r = lax.broadcasted_iota(jnp.int32, (kt, kt), 0)
    tri_c = lax.broadcasted_iota(jnp.int32, (kt, kt), 1)
    ltri_scr[...] = jnp.where(tri_c < tri_r, 1.0, 0.0).astype(BF16)

    q_chunk = jnp.right_shift(i * qb + lax.broadcasted_iota(jnp.int32, (kt, qb), 1), CHUNK_SHIFT)

    def score_tile(t, masked):
        off = pl.multiple_of(t * kt, kt)
        lg = jnp.dot(ik_ref[pl.ds(off, kt), :], iqt_scr[...], preferred_element_type=F32)
        lg = jnp.maximum(lg, 0.0) * wrow_scr[0:1, :]
        sc = (lg[:, 0:qb] + lg[:, qb:2 * qb]) + (lg[:, 2 * qb:3 * qb] + lg[:, 3 * qb:4 * qb])
        if masked:
            k_chunk = jnp.right_shift(off + lax.broadcasted_iota(jnp.int32, (kt, qb), 0), CHUNK_SHIFT)
            sc = jnp.where(k_chunk <= q_chunk, sc, -jnp.inf)
        sc_scr[pl.ds(off, kt), :] = sc

    def p1_body(t, c):
        score_tile(t, False)
        return c

    lax.fori_loop(0, n_kt - 1, p1_body, 0)
    score_tile(n_kt - 1, True)

    n_acc = 4
    zero_acc = tuple(jnp.zeros((SUBLANES, qb), jnp.int32) for _ in range(n_acc))

    def count_rows(hit_fn):
        def body(t, accs):
            off = pl.multiple_of(t * kt, kt)
            accs = list(accs)
            for r in range(kt // SUBLANES):
                s = sc_scr[pl.ds(off + r * SUBLANES, SUBLANES), :]
                accs[r % n_acc] = accs[r % n_acc] + jnp.where(hit_fn(s), 1, 0)
            return tuple(accs)
        accs = lax.fori_loop(0, n_kt, body, zero_acc)
        tot = (accs[0] + accs[1]) + (accs[2] + accs[3])
        return jnp.sum(tot.astype(F32), axis=0, keepdims=True)

    def bit_body(it, t):
        bit = lax.shift_left(jnp.int32(1), 31 - it)
        cand = t ^ bit
        cand_f = _key_to_f32(cand)
        cnt = count_rows(lambda s: s >= cand_f)
        return jnp.where(cnt >= n_sel, cand, t)

    t_key = lax.fori_loop(0, 32, bit_body, jnp.full((1, qb), -2 ** 31, jnp.int32))
    t_key = jnp.maximum(t_key, _KEY_NEG_MAX)
    tau = _key_to_f32(t_key)
    c_gt = count_rows(lambda s: s > tau)
    need = n_sel - c_gt

    m_scr[...] = jnp.full(m_scr.shape, NEG_BIG, F32)
    l_scr[...] = jnp.zeros(l_scr.shape, F32)
    acc_scr[...] = jnp.zeros(acc_scr.shape, F32)

    def tile_offset(t):
        return pl.multiple_of(jnp.minimum(t, n_kt - 1) * kt, kt)

    def tie_ranks(t, ties_before):
        sc = sc_scr[pl.ds(tile_offset(t), kt), :]
        tie_f = jnp.where(sc == tau, 1.0, 0.0)
        rank = jnp.dot(ltri_scr[...], tie_f.astype(BF16), preferred_element_type=F32) + ties_before
        return sc, tie_f, rank

    def masked_keys(t, sc, tie_f, rank):
        sel = (sc > tau) | ((tie_f > 0.0) & (rank < need))
        open_bias = jnp.where(t < n_kt, 0.0, NEG_BIG)
        bias = jnp.where(sel, open_bias, NEG_BIG).astype(BF16)
        return jnp.concatenate([k_ref[pl.ds(tile_offset(t), kt), :], bias], axis=1)

    def accumulate(t, src, g):
        m_prev = m_scr[g]
        m_next = jnp.maximum(m_prev, jnp.max(src[g], axis=0, keepdims=True))
        alpha = jnp.exp(m_prev - m_next)
        p = jnp.exp(src[g] - m_next)
        l_scr[g] = alpha * l_scr[g] + jnp.sum(p, axis=0, keepdims=True)
        m_scr[g] = m_next
        vt_g = vt_ref[g * A_HEAD_DIM:(g + 1) * A_HEAD_DIM, pl.ds(tile_offset(t), kt)]
        pv = jnp.dot(vt_g, p.astype(BF16), preferred_element_type=F32)
        acc_scr[g] = alpha * acc_scr[g] + pv

    def step(t, ties, src, dst):
        sc, tie_f, rank = tie_ranks(t + 1, ties)
        accumulate(t, src, 0)
        lhs = masked_keys(t + 1, sc, tie_f, rank)
        dst[0] = jnp.dot(lhs, wq_scr[0], preferred_element_type=F32)
        accumulate(t, src, 1)
        dst[1] = jnp.dot(lhs, wq_scr[1], preferred_element_type=F32)
        return rank[kt - 1:kt, :] + tie_f[kt - 1:kt, :]

    def tile_pair(u, ties):
        ties = step(2 * u, ties, sa_scr, sb_scr)
        return step(2 * u + 1, ties, sb_scr, sa_scr)

    zero_ties = jnp.zeros((1, qb), F32)
    sc0, tie0, rank0 = tie_ranks(0, zero_ties)
    lhs0 = masked_keys(0, sc0, tie0, rank0)
    for g in range(A_KV_HEADS):
        sa_scr[g] = jnp.dot(lhs0, wq_scr[g], preferred_element_type=F32)
    lax.fori_loop(0, (n_kt + 1) // 2, tile_pair, rank0[kt - 1:kt, :] + tie0[kt - 1:kt, :])
    for j in range(A_HEADS // 2):
        g, hh0 = j // 2, 2 * (j % 2)
        o_t = acc_scr[g] / l_scr[g]
        pair = jnp.concatenate([o_t[:, hh0 * qb:(hh0 + 1) * qb],
                                o_t[:, (hh0 + 1) * qb:(hh0 + 2) * qb]], axis=0)
        o_ref[:, j * LANES:(j + 1) * LANES] = pair.T


def _dsa(rope_out, av_t, f32_out, batch, seq):
    qb = DSA_QB
    nqb = seq // qb
    n_sel = min(TOPK_MAX, seq // 4)
    nq4 = _GSZ * qb
    kern = functools.partial(_dsa_kernel, n_sel=n_sel)
    return pl.pallas_call(
        kern,
        grid=(batch, nqb),
        in_specs=[
            pl.BlockSpec((qb, 512), lambda b, i: (b * nqb + i, 0)),
            pl.BlockSpec((qb, 256), lambda b, i: (b * nqb + i, 2)),
            pl.BlockSpec((qb, LANES), lambda b, i: (b * nqb + i, F32_W // LANES - 1)),
            pl.BlockSpec((seq, LANES), lambda b, i: (b, 6)),
            pl.BlockSpec((seq, LANES), lambda b, i: (b, 7)),
            pl.BlockSpec((LANES, seq), lambda b, i: (0, b)),
        ],
        out_specs=pl.BlockSpec((qb, 512), lambda b, i: (b * nqb + i, 0)),
        out_shape=jax.ShapeDtypeStruct((batch * seq, 512), F32),
        scratch_shapes=[
            pltpu.VMEM((seq, qb), F32),
            pltpu.VMEM((LANES, IDX_HEADS * qb), BF16),
            pltpu.VMEM((SUBLANES, IDX_HEADS * qb), F32),
            pltpu.VMEM((A_KV_HEADS, 2 * LANES, nq4), BF16),
            pltpu.VMEM((DSA_KT, DSA_KT), BF16),
            pltpu.VMEM((A_KV_HEADS, DSA_KT, nq4), F32),
            pltpu.VMEM((A_KV_HEADS, DSA_KT, nq4), F32),
            pltpu.VMEM((A_KV_HEADS, 1, nq4), F32),
            pltpu.VMEM((A_KV_HEADS, 1, nq4), F32),
            pltpu.VMEM((A_KV_HEADS, A_HEAD_DIM, nq4), F32),
        ],
        compiler_params=pltpu.CompilerParams(
            dimension_semantics=("arbitrary", "arbitrary"), vmem_limit_bytes=VMEM_LIMIT),
        name="dsa",
    )(rope_out, rope_out, f32_out, rope_out, rope_out, av_t)


BAND_QB = 128
BAND_KB = 5
BAND_KW = BAND_QB * BAND_KB


def _band_bias_table(rel_bias):
    r = np.arange(BAND_QB)[:, None]
    c = np.arange(BAND_KW)[None, :]
    a, j = r // CHUNK, c // CHUNK
    visible = (j >= a) & (j <= a + B_LEFT_CHUNKS)
    period = BAND_QB + BAND_KW
    m = np.arange(period)
    m = np.where(m < BAND_KW, m, m - period)
    idx = np.clip(B_LEFT_CHUNKS * CHUNK - m, -B_MAX_REL, B_MAX_REL) + B_MAX_REL
    v = rel_bias.astype(F32)[:, idx]
    heads = v.shape[0]
    flat = jnp.tile(v, (1, BAND_QB))[:, :BAND_QB * (period - 1)]
    tab = flat.reshape(heads, BAND_QB, period - 1)[:, :, :BAND_KW]
    return jnp.where(jnp.asarray(visible)[None], tab, NEG_BIG)


def _band_kernel(q_ref, k0, k1, k2, k3, k4, v0, v1, v2, v3, v4, bias_ref, o_ref, k_scr, v_scr):
    qb, kw = BAND_QB, BAND_KW
    i = pl.program_id(1)
    for j, (kr, vr) in enumerate(((k0, v0), (k1, v1), (k2, v2), (k3, v3), (k4, v4))):
        k_scr[j * qb:(j + 1) * qb, :] = kr[...]
        v_scr[j * qb:(j + 1) * qb, :] = vr[...]
    key_pos = (i - (BAND_KB - 1)) * qb + lax.broadcasted_iota(jnp.int32, (qb, kw), 1)
    in_seq = key_pos >= 0
    lane = lax.broadcasted_iota(jnp.int32, (qb, LANES), 1)
    low_half = lane < B_HEAD_DIM
    scale = B_HEAD_DIM ** -0.5
    for j in range(B_HEADS // 2):
        sl = slice(j * LANES, (j + 1) * LANES)
        q2 = q_ref[:, sl].astype(F32)
        k2s = k_scr[:, sl]
        v2s = v_scr[:, sl]
        halves = []
        for par in range(2):
            keep = low_half if par == 0 else jnp.logical_not(low_half)
            qh = jnp.where(keep, q2, 0.0).astype(BF16)
            s = lax.dot_general(qh, k2s, (((1,), (1,)), ((), ())), preferred_element_type=F32)
            s = s * scale + bias_ref[2 * j + par]
            s = jnp.where(in_seq, s, NEG_BIG)
            m = jnp.max(s, axis=1, keepdims=True)
            p = jnp.exp(s - m)
            p = p / jnp.sum(p, axis=1, keepdims=True)
            halves.append(jnp.dot(p.astype(BF16), v2s, preferred_element_type=F32))
        o_ref[:, sl] = jnp.where(low_half, halves[0], halves[1])


def _band(bf_out, bias_tab, batch, seq):
    qb = BAND_QB
    nqb = seq // qb

    def kv_spec(col, j):
        return pl.BlockSpec(
            (qb, 512), lambda b, i: (b * nqb + jnp.maximum(i - (BAND_KB - 1) + j, 0), col))

    return pl.pallas_call(
        _band_kernel,
        grid=(batch, nqb),
        in_specs=([pl.BlockSpec((qb, 512), lambda b, i: (b * nqb + i, 0))]
                  + [kv_spec(1, j) for j in range(BAND_KB)]
                  + [kv_spec(2, j) for j in range(BAND_KB)]
                  + [pl.BlockSpec((B_HEADS, qb, BAND_KW), lambda b, i: (0, 0, 0))]),
        out_specs=pl.BlockSpec((qb, 512), lambda b, i: (b * nqb + i, 0)),
        out_shape=jax.ShapeDtypeStruct((batch * seq, 512), F32),
        scratch_shapes=[pltpu.VMEM((BAND_KW, 512), BF16), pltpu.VMEM((BAND_KW, 512), BF16)],
        compiler_params=pltpu.CompilerParams(
            dimension_semantics=("arbitrary", "arbitrary"), vmem_limit_bytes=VMEM_LIMIT),
        name="band",
    )(bf_out, *([bf_out] * (2 * BAND_KB)), bias_tab)


_IG_LANE = IDX_HEADS
_FG_LANE = IDX_HEADS + C_HEADS


def _log_sigmoid(x):
    return jnp.minimum(x, 0.0) - jnp.log1p(jnp.exp(-jnp.abs(x)))


def _mlstm_kernel(qk_ref, v_ref, o_ref, sm_ref, fb_ref, cw_ref, cb_ref, out_ref,
                  xx_scr, c_scr, n_scr, m_scr, tr_scr):
    L, d = CHUNK, C_HEAD_DIM
    c_idx = pl.program_id(1)

    @pl.when(c_idx == 0)
    def _():
        xx_scr[0:8, :] = jnp.zeros((8, xx_scr.shape[1]), F32)
        c_scr[...] = jnp.zeros(c_scr.shape, F32)
        n_scr[...] = jnp.zeros(n_scr.shape, F32)
        m_scr[...] = jnp.zeros(m_scr.shape, F32)

    x = qk_ref[...]
    xx_scr[8:8 + L, :] = x
    y = cb_ref[...]
    for j in range(C_CONV):
        y = y + cw_ref[j:j + 1, :] * xx_scr[8 - (C_CONV - 1) + j:8 - (C_CONV - 1) + j + L, :]
    xx_scr[0:8, :] = x[L - 8:L, :]
    qk = y * jax.nn.sigmoid(y)

    a = sm_ref[...] + fb_ref[...]
    logf = _log_sigmoid(a)
    row = lax.broadcasted_iota(jnp.int32, (L, L), 0)
    col = lax.broadcasted_iota(jnp.int32, (L, L), 1)
    causal = col <= row
    tri = jnp.where(causal, 1.0, 0.0).astype(F32)
    bcol = jnp.dot(tri, logf, preferred_element_type=F32, precision=lax.Precision.HIGHEST)
    lane = lax.broadcasted_iota(jnp.int32, (L, LANES), 1)
    mix = jnp.where(lane >= _FG_LANE, bcol, a)
    tr_scr[0:L, :] = mix
    tr_scr[L:2 * L, :] = jnp.zeros((L, LANES), F32)
    rows_t = jnp.transpose(tr_scr[...])

    for h in range(C_HEADS):
        sl = slice(h * d, (h + 1) * d)
        q = qk[:, sl] * (d ** -0.5)
        k = qk[:, C_HEADS * d + h * d:C_HEADS * d + (h + 1) * d]
        v = v_ref[:, sl]
        qb16 = q.astype(BF16)
        b_col = bcol[:, _FG_LANE + h:_FG_LANE + h + 1]
        i_col = a[:, _IG_LANE + h:_IG_LANE + h + 1]
        b_row = rows_t[_FG_LANE + h:_FG_LANE + h + 1, 0:L]
        i_row = rows_t[_IG_LANE + h:_IG_LANE + h + 1, 0:L]
        g = b_col[L - 1:L, :]
        m_prev = m_scr[h][:, 0:1]
        c_prev = c_scr[h]
        n_prev = n_scr[h]

        dmat = jnp.where(causal, b_col - b_row + i_row, NEG_BIG)
        m_inter = b_col + m_prev
        m_row = jnp.maximum(m_inter, jnp.max(dmat, axis=1, keepdims=True))
        inter = jnp.exp(m_inter - m_row)
        s_qk = lax.dot_general(qb16, k.astype(BF16), (((1,), (1,)), ((), ())), preferred_element_type=F32)
        s_qk = s_qk * jnp.exp(dmat - m_row)
        num = (jnp.dot(s_qk.astype(BF16), v, preferred_element_type=F32)
               + inter * jnp.dot(qb16, c_prev.astype(BF16), preferred_element_type=F32))
        den = jnp.sum(s_qk, axis=1, keepdims=True) + inter * jnp.sum(q * n_prev, axis=1, keepdims=True)
        hval = num / jnp.maximum(jnp.abs(den), jnp.exp(-m_row))
        out_ref[:, sl] = jax.nn.sigmoid(o_ref[:, sl]) * hval

        w_log = g - b_col + i_col
        m_loc = jnp.max(w_log, axis=0, keepdims=True)
        kw = k * jnp.exp(w_log - m_loc)
        kv = lax.dot_general(kw.astype(BF16), v, (((0,), (0,)), ((), ())), preferred_element_type=F32)
        ksum = jnp.sum(kw, axis=0, keepdims=True)
        m_new = jnp.maximum(g + m_prev, m_loc)
        fa = jnp.exp(g + m_prev - m_new)
        fb = jnp.exp(m_loc - m_new)
        c_scr[h] = fa * c_prev + fb * kv
        n_scr[h] = fa * n_prev + fb * ksum
        m_scr[h] = jnp.broadcast_to(m_new, (1, LANES))


def _mlstm(f32_out, bf_out, fbias_row, conv_w, conv_b, batch, seq):
    L = CHUNK
    nc = seq // L
    return pl.pallas_call(
        _mlstm_kernel,
        grid=(batch, nc),
        in_specs=[
            pl.BlockSpec((L, 1024), lambda b, c: (b * nc + c, 5)),
            pl.BlockSpec((L, 512), lambda b, c: (b * nc + c, 3)),
            pl.BlockSpec((L, 512), lambda b, c: (b * nc + c, 9)),
            pl.BlockSpec((L, LANES), lambda b, c: (b * nc + c, F32_W // LANES - 1)),
            pl.BlockSpec((1, LANES), lambda b, c: (0, 0)),
            pl.BlockSpec((C_CONV, 1024), lambda b, c: (0, 0)),
            pl.BlockSpec((1, 1024), lambda b, c: (0, 0)),
        ],
        out_specs=pl.BlockSpec((L, 512), lambda b, c: (b * nc + c, 0)),
        out_shape=jax.ShapeDtypeStruct((batch * seq, 512), F32),
        scratch_shapes=[
            pltpu.VMEM((8 + L, 1024), F32),
            pltpu.VMEM((C_HEADS, C_HEAD_DIM, C_HEAD_DIM), F32),
            pltpu.VMEM((C_HEADS, 1, C_HEAD_DIM), F32),
            pltpu.VMEM((C_HEADS, 1, LANES), F32),
            pltpu.VMEM((2 * L, LANES), F32),
        ],
        compiler_params=pltpu.CompilerParams(
            dimension_semantics=("arbitrary", "arbitrary"), vmem_limit_bytes=VMEM_LIMIT),
        name="mlstm",
    )(f32_out, bf_out, f32_out, f32_out, fbias_row, conv_w, conv_b)


def _silu(x):
    return x * jax.nn.sigmoid(x)


def _merge_kernel(x_ref, ya_ref, yb_ref, yc_ref, gate_ref, z_ref, cn_ref, wbr_ref, wout_ref, fg_ref,
                  o_ref, *, final):
    d = C_HEAD_DIM
    yc = yc_ref[...]
    parts = []
    for h in range(C_HEADS):
        hv = yc[:, h * d:(h + 1) * d]
        hc = hv - jnp.mean(hv, axis=-1, keepdims=True)
        parts.append(hc * lax.rsqrt(jnp.mean(hc * hc, axis=-1, keepdims=True) + NORM_EPS))
    yc_n = jnp.concatenate(parts, axis=1) * cn_ref[...]
    merged = None
    for n, y in enumerate((ya_ref[...], yb_ref[...], yc_n)):
        z = z_ref[:, n * BRANCH_WIDTH:(n + 1) * BRANCH_WIDTH]
        br = (y * _silu(z)).astype(BF16)
        proj = jnp.dot(br, wbr_ref[n], preferred_element_type=F32)
        term = jax.nn.sigmoid(gate_ref[:, n * D_MODEL:(n + 1) * D_MODEL]) * proj
        merged = term if merged is None else merged + term
    out = x_ref[...] + jnp.dot(merged.astype(BF16), wout_ref[...], preferred_element_type=F32)
    if final:
        ms = jnp.mean(out * out, axis=-1, keepdims=True)
        out = out * lax.rsqrt(ms + NORM_EPS) * fg_ref[...]
    o_ref[...] = out


def _merge(x2d, ya, yb, yc, f32_out, cn, wbr, wout, fg, final, tm=256):
    t, d = x2d.shape
    kern = functools.partial(_merge_kernel, final=final)
    return pl.pallas_call(
        kern,
        grid=(t // tm,),
        in_specs=[
            pl.BlockSpec((tm, d), lambda i: (i, 0)),
            pl.BlockSpec((tm, 512), lambda i: (i, 0)),
            pl.BlockSpec((tm, 512), lambda i: (i, 0)),
            pl.BlockSpec((tm, 512), lambda i: (i, 0)),
            pl.BlockSpec((tm, 3 * D_MODEL), lambda i: (i, 0)),
            pl.BlockSpec((tm, 3 * BRANCH_WIDTH), lambda i: (i, 2)),
            pl.BlockSpec((1, 512), lambda i: (0, 0)),
            pl.BlockSpec((N_BRANCH, BRANCH_WIDTH, d), lambda i: (0, 0, 0)),
            pl.BlockSpec((d, d), lambda i: (0, 0)),
            pl.BlockSpec((1, d), lambda i: (0, 0)),
        ],
        out_specs=pl.BlockSpec((tm, d), lambda i: (i, 0)),
        out_shape=jax.ShapeDtypeStruct((t, d), F32),
        compiler_params=pltpu.CompilerParams(
            dimension_semantics=("arbitrary",), vmem_limit_bytes=VMEM_LIMIT),
        name="merge",
    )(x2d, ya, yb, yc, f32_out, f32_out, cn, wbr, wout, fg)


def _layer(x2d, batch, seq, ctab, stab, norm_g, w_in, b_in, f_bias, conv_w, conv_b, rel_bias,
           c_norm_w, w_branch, w_out, final_g, final):
    g = norm_g[None, :]
    b_row = b_in[None, :]
    w_rope = _take_cols(w_in, _ROPE_NAMES, ROPE_W)
    b_rope = _take_cols(b_row, _ROPE_NAMES, ROPE_W)
    rope_out = _proj_rope(x2d, g, w_rope.astype(BF16), _rope_partner(w_rope).astype(BF16),
                          b_rope, _rope_partner(b_rope), ctab, stab, seq)
    bf_out, av_t = _proj_bf(x2d, g, _take_cols(w_in, _BF_NAMES, BF_W).astype(BF16),
                            _take_cols(b_row, _BF_NAMES, BF_W))
    f32_out = _proj(x2d, g, _take_cols(w_in, _F32_NAMES, F32_W).astype(BF16),
                    _take_cols(b_row, _F32_NAMES, F32_W), F32, F32_TN, tm=min(1024, seq))

    y_a = _dsa(rope_out, av_t, f32_out, batch, seq)
    y_b = _band(bf_out, _band_bias_table(rel_bias), batch, seq)
    fbias_row = jnp.zeros((1, LANES), F32).at[0, _FG_LANE:_FG_LANE + C_HEADS].set(f_bias)
    y_c = _mlstm(f32_out, bf_out, fbias_row, conv_w, conv_b[None, :], batch, seq)
    return _merge(x2d, y_a, y_b, y_c, f32_out, c_norm_w[None, :], w_branch.astype(BF16),
                  w_out.astype(BF16), final_g[None, :], final)


def kernel(x, norm_g, w_in, b_in, f_bias, conv_w, conv_b, rel_bias, c_norm_w, w_branch, w_out, final_g):
    batch, seq, d = x.shape
    ctab, stab = _rope_tables(seq)
    x2d = x.reshape(batch * seq, d)
    depth = norm_g.shape[0]
    for l in range(depth):
        x2d = _layer(x2d, batch, seq, ctab, stab, norm_g[l], w_in[l], b_in[l], f_bias[l], conv_w[l],
                     conv_b[l], rel_bias[l], c_norm_w[l], w_branch[l], w_out[l], final_g, l == depth - 1)
    return x2d.reshape(batch, seq, d)
```

```python
import functools

import numpy as np
import jax
import jax.numpy as jnp
from jax import lax
from jax.experimental import pallas as pl
from jax.experimental.pallas import tpu as pltpu

F32 = jnp.float32
BF16 = jnp.bfloat16

D_MODEL = 1024
DEPTH = 2
CHUNK = 64
CHUNK_SHIFT = 6
NORM_EPS = 1e-6
ROPE_THETA = 500000.0

A_HEADS = 8
A_KV_HEADS = 2
A_HEAD_DIM = 64
IDX_HEADS = 4
IDX_DIM = 64
TOPK_MAX = 256
ROT_DIM = A_HEAD_DIM // 4
B_HEADS = 8
B_HEAD_DIM = 64
B_LEFT_CHUNKS = 8
B_MAX_REL = 128
C_HEADS = 4
C_HEAD_DIM = 128
C_CONV = 4
N_BRANCH = 3
BRANCH_WIDTH = 512

IN_NAMES = ('a_q', 'a_k', 'a_v', 'i_q', 'i_k', 'i_w', 'a_z', 'b_q', 'b_k', 'b_v', 'b_z',
            'c_q', 'c_k', 'c_v', 'c_i', 'c_f', 'c_o', 'c_z', 'gates')
IN_SPLITS = (512, 128, 128, 256, 64, 4, 512, 512, 512, 512, 512,
             512, 512, 512, 4, 4, 512, 512, 3 * D_MODEL)
_OFF = dict(zip(IN_NAMES, np.concatenate([[0], np.cumsum(IN_SPLITS)[:-1]]).tolist()))
_WID = dict(zip(IN_NAMES, IN_SPLITS))

LANES = 128
SUBLANES = 8
NEG_BIG = -1e30
VMEM_LIMIT = 56 * 1024 * 1024


_ROPE_NAMES = ('a_q', 'i_q', 'a_k', 'i_k')
ROPE_W = 1024
_BF_NAMES = ('b_q', 'b_k', 'b_v', 'c_v', 'a_v')
BF_W = 2176
_AV_OFF = 2048
_F32_NAMES = ('gates', 'a_z', 'b_z', 'c_z', 'c_o', 'c_q', 'c_k', 'i_w', 'c_i', 'c_f')
F32_W = 6272
F32_TN = 896


def _take_cols(a, names, width):
    idx = np.concatenate([np.arange(_OFF[n], _OFF[n] + _WID[n]) for n in names])
    out = a[..., idx]
    return jnp.pad(out, [(0, 0)] * (a.ndim - 1) + [(0, width - idx.shape[0])])


def _rope_partner(a):
    half = ROT_DIM // 2
    heads = a.reshape(a.shape[:-1] + (a.shape[-1] // A_HEAD_DIM, A_HEAD_DIM))
    swapped = jnp.concatenate([heads[..., half:ROT_DIM], heads[..., :half], heads[..., ROT_DIM:]], axis=-1)
    return swapped.reshape(a.shape)


def _normed_bf16(x_ref, g_ref):
    x = x_ref[...]
    ms = jnp.mean(x * x, axis=-1, keepdims=True)
    return (x * lax.rsqrt(ms + NORM_EPS) * g_ref[...]).astype(BF16)


def _proj_kernel(x_ref, g_ref, w_ref, b_ref, o_ref, h_scr):
    @pl.when(pl.program_id(1) == 0)
    def _():
        h_scr[...] = _normed_bf16(x_ref, g_ref)

    acc = jnp.dot(h_scr[...], w_ref[...], preferred_element_type=F32) + b_ref[...]
    o_ref[...] = acc.astype(o_ref.dtype)


def _proj_bf_kernel(x_ref, g_ref, w_ref, b_ref, o_ref, vt_ref):
    h = _normed_bf16(x_ref, g_ref)
    acc = jnp.dot(h, w_ref[...], preferred_element_type=F32) + b_ref[...]
    o_ref[...] = acc.astype(o_ref.dtype)
    for r in range(acc.shape[0] // LANES):
        blk = acc[r * LANES:(r + 1) * LANES, _AV_OFF:_AV_OFF + LANES]
        vt_ref[:, r * LANES:(r + 1) * LANES] = blk.T.astype(vt_ref.dtype)


def _proj_rope_kernel(x_ref, g_ref, w_ref, wp_ref, b_ref, bp_ref, c_ref, s_ref, o_ref):
    h = _normed_bf16(x_ref, g_ref)
    acc = jnp.dot(h, w_ref[...], preferred_element_type=F32) + b_ref[...]
    accp = jnp.dot(h, wp_ref[...], preferred_element_type=F32) + bp_ref[...]
    reps = acc.shape[1] // LANES
    c = jnp.concatenate([c_ref[...]] * reps, axis=1)
    s = jnp.concatenate([s_ref[...]] * reps, axis=1)
    o_ref[...] = (acc * c + accp * s).astype(o_ref.dtype)


def _proj(x2d, g, w, b, out_dtype, tn, tm=512):
    t, d = x2d.shape
    n = w.shape[1]
    return pl.pallas_call(
        _proj_kernel,
        grid=(t // tm, n // tn),
        in_specs=[
            pl.BlockSpec((tm, d), lambda i, j: (i, 0)),
            pl.BlockSpec((1, d), lambda i, j: (0, 0)),
            pl.BlockSpec((d, tn), lambda i, j: (0, j)),
            pl.BlockSpec((1, tn), lambda i, j: (0, j)),
        ],
        out_specs=pl.BlockSpec((tm, tn), lambda i, j: (i, j)),
        out_shape=jax.ShapeDtypeStruct((t, n), out_dtype),
        scratch_shapes=[pltpu.VMEM((tm, d), BF16)],
        compiler_params=pltpu.CompilerParams(
            dimension_semantics=("arbitrary", "arbitrary"), vmem_limit_bytes=VMEM_LIMIT),
        name="proj",
    )(x2d, g, w, b)


def _proj_bf(x2d, g, w, b, tm=512):
    t, d = x2d.shape
    n = w.shape[1]
    return pl.pallas_call(
        _proj_bf_kernel,
        grid=(t // tm,),
        in_specs=[
            pl.BlockSpec((tm, d), lambda i: (i, 0)),
            pl.BlockSpec((1, d), lambda i: (0, 0)),
            pl.BlockSpec((d, n), lambda i: (0, 0)),
            pl.BlockSpec((1, n), lambda i: (0, 0)),
        ],
        out_specs=[pl.BlockSpec((tm, n), lambda i: (i, 0)),
                   pl.BlockSpec((LANES, tm), lambda i: (0, i))],
        out_shape=[jax.ShapeDtypeStruct((t, n), BF16), jax.ShapeDtypeStruct((LANES, t), BF16)],
        compiler_params=pltpu.CompilerParams(
            dimension_semantics=("arbitrary",), vmem_limit_bytes=VMEM_LIMIT),
        name="proj_bf",
    )(x2d, g, w, b)


def _proj_rope(x2d, g, w, wp, b, bp, ctab, stab, seq, tm=512):
    t, d = x2d.shape
    n = w.shape[1]
    nsb = seq // tm
    return pl.pallas_call(
        _proj_rope_kernel,
        grid=(t // tm,),
        in_specs=[
            pl.BlockSpec((tm, d), lambda i: (i, 0)),
            pl.BlockSpec((1, d), lambda i: (0, 0)),
            pl.BlockSpec((d, n), lambda i: (0, 0)),
            pl.BlockSpec((d, n), lambda i: (0, 0)),
            pl.BlockSpec((1, n), lambda i: (0, 0)),
            pl.BlockSpec((1, n), lambda i: (0, 0)),
            pl.BlockSpec((tm, LANES), lambda i: (i % nsb, 0)),
            pl.BlockSpec((tm, LANES), lambda i: (i % nsb, 0)),
        ],
        out_specs=pl.BlockSpec((tm, n), lambda i: (i, 0)),
        out_shape=jax.ShapeDtypeStruct((t, n), BF16),
        compiler_params=pltpu.CompilerParams(
            dimension_semantics=("arbitrary",), vmem_limit_bytes=VMEM_LIMIT),
        name="proj_rope",
    )(x2d, g, w, wp, b, bp, ctab, stab)


def _rope_tables(seq):
    half = ROT_DIM // 2
    inv_freq = 1.0 / (ROPE_THETA ** (jnp.arange(half, dtype=F32) * 2.0 / ROT_DIM))
    ang = jnp.arange(seq, dtype=F32)[:, None] * inv_freq[None, :]
    cos, sin = jnp.cos(ang), jnp.sin(ang)
    ones = jnp.ones((seq, A_HEAD_DIM - ROT_DIM), F32)
    zeros = jnp.zeros((seq, A_HEAD_DIM - ROT_DIM), F32)
    c64 = jnp.concatenate([cos, cos, ones], axis=1)
    s64 = jnp.concatenate([-sin, sin, zeros], axis=1)
    return jnp.tile(c64, (1, LANES // A_HEAD_DIM)), jnp.tile(s64, (1, LANES // A_HEAD_DIM))


DSA_QB = 128
DSA_KT = 512
_KEY_NEG_MAX = -2139095040
_GSZ = A_HEADS // A_KV_HEADS


def _key_to_f32(t):
    bits = jnp.where(t >= 0, t, t ^ jnp.int32(0x7FFFFFFF))
    return pltpu.bitcast(bits, F32)


def _dsa_kernel(q_ref, iq_ref, iw_ref, k_ref, ik_ref, vt_ref, o_ref,
                sc_scr, iqt_scr, wrow_scr, wq_scr, ltri_scr, sa_scr, sb_scr, m_scr, l_scr, acc_scr,
                *, n_sel):
    qb, kt = DSA_QB, DSA_KT
    nq4 = _GSZ * qb
    i = pl.program_id(1)
    n_kt = ((i + 1) * qb + kt - 1) // kt
    lane = lax.broadcasted_iota(jnp.int32, (qb, LANES), 1)
    low_half = lane < A_HEAD_DIM

    for h in range(IDX_HEADS):
        slab = iq_ref[:, (h // 2) * LANES:(h // 2 + 1) * LANES].astype(F32)
        if h % 2 == 1:
            slab = pltpu.roll(slab, A_HEAD_DIM, 1)
        iqt_scr[:, h * qb:(h + 1) * qb] = jnp.where(low_half, slab, 0.0).T.astype(BF16)
    iw_t = iw_ref[...].T
    for h in range(IDX_HEADS):
        wrow_scr[0:1, h * qb:(h + 1) * qb] = iw_t[h:h + 1, :] * (IDX_DIM ** -0.5 * IDX_HEADS ** -0.5)
    eye = jnp.where(lax.broadcasted_iota(jnp.int32, (qb, LANES), 0) == lane, 1.0, 0.0).astype(BF16)
    for h in range(A_HEADS):
        g, hh = h // _GSZ, h % _GSZ
        slab = q_ref[:, (h // 2) * LANES:(h // 2 + 1) * LANES].astype(F32)
        if (h % 2) != g:
            slab = pltpu.roll(slab, A_HEAD_DIM, 1)
        keep = low_half if g == 0 else jnp.logical_not(low_half)
        slab = jnp.where(keep, slab, 0.0) * (A_HEAD_DIM ** -0.5)
        wq_scr[g, 0:LANES, hh * qb:(hh + 1) * qb] = slab.T.astype(BF16)
        wq_scr[g, LANES:2 * LANES, hh * qb:(hh + 1) * qb] = eye
    tri_r = lax.broadcasted_iota(jnp.int32, (kt, kt), 0)
    tri_c = lax.broadcasted_iota(jnp.int32, (kt, kt), 1)
    ltri_scr[...] = jnp.where(tri_c < tri_r, 1.0, 0.0).astype(BF16)

    q_chunk = jnp.right_shift(i * qb + lax.broadcasted_iota(jnp.int32, (kt, qb), 1), CHUNK_SHIFT)

    def score_tile(t, masked):
        off = pl.multiple_of(t * kt, kt)
        lg = jnp.dot(ik_ref[pl.ds(off, kt), :], iqt_scr[...], preferred_element_type=F32)
        lg = jnp.maximum(lg, 0.0) * wrow_scr[0:1, :]
        sc = (lg[:, 0:qb] + lg[:, qb:2 * qb]) + (lg[:, 2 * qb:3 * qb] + lg[:, 3 * qb:4 * qb])
        if masked:
            k_chunk = jnp.right_shift(off + lax.broadcasted_iota(jnp.int32, (kt, qb), 0), CHUNK_SHIFT)
            sc = jnp.where(k_chunk <= q_chunk, sc, -jnp.inf)
        sc_scr[pl.ds(off, kt), :] = sc

    def p1_body(t, c):
        score_tile(t, False)
        return c

    lax.fori_loop(0, n_kt - 1, p1_body, 0)
    score_tile(n_kt - 1, True)

    n_acc = 4
    zero_acc = tuple(jnp.zeros((SUBLANES, qb), jnp.int32) for _ in range(n_acc))

    def count_rows(hit_fn):
        def body(t, accs):
            off = pl.multiple_of(t * kt, kt)
            accs = list(accs)
            for r in range(kt // SUBLANES):
                s = sc_scr[pl.ds(off + r * SUBLANES, SUBLANES), :]
                accs[r % n_acc] = accs[r % n_acc] + jnp.where(hit_fn(s), 1, 0)
            return tuple(accs)
        accs = lax.fori_loop(0, n_kt, body, zero_acc)
        tot = (accs[0] + accs[1]) + (accs[2] + accs[3])
        return jnp.sum(tot.astype(F32), axis=0, keepdims=True)

    def bit_body(it, t):
        bit = lax.shift_left(jnp.int32(1), 31 - it)
        cand = t ^ bit
        cand_f = _key_to_f32(cand)
        cnt = count_rows(lambda s: s >= cand_f)
        return jnp.where(cnt >= n_sel, cand, t)

    t_key = lax.fori_loop(0, 32, bit_body, jnp.full((1, qb), -2 ** 31, jnp.int32))
    t_key = jnp.maximum(t_key, _KEY_NEG_MAX)
    tau = _key_to_f32(t_key)
    c_gt = count_rows(lambda s: s > tau)
    need = n_sel - c_gt

    m_scr[...] = jnp.full(m_scr.shape, NEG_BIG, F32)
    l_scr[...] = jnp.zeros(l_scr.shape, F32)
    acc_scr[...] = jnp.zeros(acc_scr.shape, F32)

    def tile_offset(t):
        return pl.multiple_of(jnp.minimum(t, n_kt - 1) * kt, kt)

    def tie_ranks(t, ties_before):
        sc = sc_scr[pl.ds(tile_offset(t), kt), :]
        tie_f = jnp.where(sc == tau, 1.0, 0.0)
        rank = jnp.dot(ltri_scr[...], tie_f.astype(BF16), preferred_element_type=F32) + ties_before
        return sc, tie_f, rank

    def masked_keys(t, sc, tie_f, rank):
        sel = (sc > tau) | ((tie_f > 0.0) & (rank < need))
        open_bias = jnp.where(t < n_kt, 0.0, NEG_BIG)
        bias = jnp.where(sel, open_bias, NEG_BIG).astype(BF16)
        return jnp.concatenate([k_ref[pl.ds(tile_offset(t), kt), :], bias], axis=1)

    def accumulate(t, src, g):
        m_prev = m_scr[g]
        m_next = jnp.maximum(m_prev, jnp.max(src[g], axis=0, keepdims=True))
        alpha = jnp.exp(m_prev - m_next)
        p = jnp.exp(src[g] - m_next)
        l_scr[g] = alpha * l_scr[g] + jnp.sum(p, axis=0, keepdims=True)
        m_scr[g] = m_next
        vt_g = vt_ref[g * A_HEAD_DIM:(g + 1) * A_HEAD_DIM, pl.ds(tile_offset(t), kt)]
        pv = jnp.dot(vt_g, p.astype(BF16), preferred_element_type=F32)
        acc_scr[g] = alpha * acc_scr[g] + pv

    def step(t, ties, src, dst):
        sc, tie_f, rank = tie_ranks(t + 1, ties)
        accumulate(t, src, 0)
        lhs = masked_keys(t + 1, sc, tie_f, rank)
        dst[0] = jnp.dot(lhs, wq_scr[0], preferred_element_type=F32)
        accumulate(t, src, 1)
        dst[1] = jnp.dot(lhs, wq_scr[1], preferred_element_type=F32)
        return rank[kt - 1:kt, :] + tie_f[kt - 1:kt, :]

    def tile_pair(u, ties):
        ties = step(2 * u, ties, sa_scr, sb_scr)
        return step(2 * u + 1, ties, sb_scr, sa_scr)

    zero_ties = jnp.zeros((1, qb), F32)
    sc0, tie0, rank0 = tie_ranks(0, zero_ties)
    lhs0 = masked_keys(0, sc0, tie0, rank0)
    for g in range(A_KV_HEADS):
        sa_scr[g] = jnp.dot(lhs0, wq_scr[g], preferred_element_type=F32)
    lax.fori_loop(0, (n_kt + 1) // 2, tile_pair, rank0[kt - 1:kt, :] + tie0[kt - 1:kt, :])
    for j in range(A_HEADS // 2):
        g, hh0 = j // 2, 2 * (j % 2)
        o_t = acc_scr[g] / l_scr[g]
        pair = jnp.concatenate([o_t[:, hh0 * qb:(hh0 + 1) * qb],
                                o_t[:, (hh0 + 1) * qb:(hh0 + 2) * qb]], axis=0)
        o_ref[:, j * LANES:(j + 1) * LANES] = pair.T


def _dsa(rope_out, av_t, f32_out, batch, seq):
    qb = DSA_QB
    nqb = seq // qb
    n_sel = min(TOPK_MAX, seq // 4)
    nq4 = _GSZ * qb
    kern = functools.partial(_dsa_kernel, n_sel=n_sel)
    return pl.pallas_call(
        kern,
        grid=(batch, nqb),
        in_specs=[
            pl.BlockSpec((qb, 512), lambda b, i: (b * nqb + i, 0)),
            pl.BlockSpec((qb, 256), lambda b, i: (b * nqb + i, 2)),
            pl.BlockSpec((qb, LANES), lambda b, i: (b * nqb + i, F32_W // LANES - 1)),
            pl.BlockSpec((seq, LANES), lambda b, i: (b, 6)),
            pl.BlockSpec((seq, LANES), lambda b, i: (b, 7)),
            pl.BlockSpec((LANES, seq), lambda b, i: (0, b)),
        ],
        out_specs=pl.BlockSpec((qb, 512), lambda b, i: (b * nqb + i, 0)),
        out_shape=jax.ShapeDtypeStruct((batch * seq, 512), F32),
        scratch_shapes=[
            pltpu.VMEM((seq, qb), F32),
            pltpu.VMEM((LANES, IDX_HEADS * qb), BF16),
            pltpu.VMEM((SUBLANES, IDX_HEADS * qb), F32),
            pltpu.VMEM((A_KV_HEADS, 2 * LANES, nq4), BF16),
            pltpu.VMEM((DSA_KT, DSA_KT), BF16),
            pltpu.VMEM((A_KV_HEADS, DSA_KT, nq4), F32),
            pltpu.VMEM((A_KV_HEADS, DSA_KT, nq4), F32),
            pltpu.VMEM((A_KV_HEADS, 1, nq4), F32),
            pltpu.VMEM((A_KV_HEADS, 1, nq4), F32),
            pltpu.VMEM((A_KV_HEADS, A_HEAD_DIM, nq4), F32),
        ],
        compiler_params=pltpu.CompilerParams(
            dimension_semantics=("arbitrary", "arbitrary"), vmem_limit_bytes=VMEM_LIMIT),
        name="dsa",
    )(rope_out, rope_out, f32_out, rope_out, rope_out, av_t)


BAND_QB = 128
BAND_KB = 5
BAND_KW = BAND_QB * BAND_KB


def _band_bias_table(rel_bias):
    r = np.arange(BAND_QB)[:, None]
    c = np.arange(BAND_KW)[None, :]
    a, j = r // CHUNK, c // CHUNK
    visible = (j >= a) & (j <= a + B_LEFT_CHUNKS)
    period = BAND_QB + BAND_KW
    m = np.arange(period)
    m = np.where(m < BAND_KW, m, m - period)
    idx = np.clip(B_LEFT_CHUNKS * CHUNK - m, -B_MAX_REL, B_MAX_REL) + B_MAX_REL
    v = rel_bias.astype(F32)[:, idx]
    heads = v.shape[0]
    flat = jnp.tile(v, (1, BAND_QB))[:, :BAND_QB * (period - 1)]
    tab = flat.reshape(heads, BAND_QB, period - 1)[:, :, :BAND_KW]
    return jnp.where(jnp.asarray(visible)[None], tab, NEG_BIG)


def _band_kernel(q_ref, k0, k1, k2, k3, k4, v0, v1, v2, v3, v4, bias_ref, o_ref, k_scr, v_scr):
    qb, kw = BAND_QB, BAND_KW
    i = pl.program_id(1)
    for j, (kr, vr) in enumerate(((k0, v0), (k1, v1), (k2, v2), (k3, v3), (k4, v4))):
        k_scr[j * qb:(j + 1) * qb, :] = kr[...]
        v_scr[j * qb:(j + 1) * qb, :] = vr[...]
    key_pos = (i - (BAND_KB - 1)) * qb + lax.broadcasted_iota(jnp.int32, (qb, kw), 1)
    in_seq = key_pos >= 0
    lane = lax.broadcasted_iota(jnp.int32, (qb, LANES), 1)
    low_half = lane < B_HEAD_DIM
    scale = B_HEAD_DIM ** -0.5
    for j in range(B_HEADS // 2):
        sl = slice(j * LANES, (j + 1) * LANES)
        q2 = q_ref[:, sl].astype(F32)
        k2s = k_scr[:, sl]
        v2s = v_scr[:, sl]
        halves = []
        for par in range(2):
            keep = low_half if par == 0 else jnp.logical_not(low_half)
            qh = (jnp.where(keep, q2, 0.0) * scale).astype(BF16)
            s = lax.dot_general(qh, k2s, (((1,), (1,)), ((), ())), preferred_element_type=F32)
            s = jnp.where(in_seq, s + bias_ref[2 * j + par], NEG_BIG)
            m = jnp.max(s, axis=1, keepdims=True)
            p = jnp.exp(s - m)
            denom = jnp.sum(p, axis=1, keepdims=True)
            halves.append(jnp.dot(p.astype(BF16), v2s, preferred_element_type=F32) / denom)
        o_ref[:, sl] = jnp.where(low_half, halves[0], halves[1])


def _band(bf_out, bias_tab, batch, seq):
    qb = BAND_QB
    nqb = seq // qb

    def kv_spec(col, j):
        return pl.BlockSpec(
            (qb, 512), lambda b, i: (b * nqb + jnp.maximum(i - (BAND_KB - 1) + j, 0), col))

    return pl.pallas_call(
        _band_kernel,
        grid=(batch, nqb),
        in_specs=([pl.BlockSpec((qb, 512), lambda b, i: (b * nqb + i, 0))]
                  + [kv_spec(1, j) for j in range(BAND_KB)]
                  + [kv_spec(2, j) for j in range(BAND_KB)]
                  + [pl.BlockSpec((B_HEADS, qb, BAND_KW), lambda b, i: (0, 0, 0))]),
        out_specs=pl.BlockSpec((qb, 512), lambda b, i: (b * nqb + i, 0)),
        out_shape=jax.ShapeDtypeStruct((batch * seq, 512), F32),
        scratch_shapes=[pltpu.VMEM((BAND_KW, 512), BF16), pltpu.VMEM((BAND_KW, 512), BF16)],
        compiler_params=pltpu.CompilerParams(
            dimension_semantics=("arbitrary", "arbitrary"), vmem_limit_bytes=VMEM_LIMIT),
        name="band",
    )(bf_out, *([bf_out] * (2 * BAND_KB)), bias_tab)


_IG_LANE = IDX_HEADS
_FG_LANE = IDX_HEADS + C_HEADS


def _log_sigmoid(x):
    return jnp.minimum(x, 0.0) - jnp.log1p(jnp.exp(-jnp.abs(x)))


def _mlstm_kernel(qk_ref, v_ref, o_ref, sm_ref, fb_ref, cw_ref, cb_ref, out_ref,
                  xx_scr, c_scr, n_scr, m_scr, tr_scr):
    L, d = CHUNK, C_HEAD_DIM
    c_idx = pl.program_id(1)

    @pl.when(c_idx == 0)
    def _():
        xx_scr[0:8, :] = jnp.zeros((8, xx_scr.shape[1]), F32)
        c_scr[...] = jnp.zeros(c_scr.shape, F32)
        n_scr[...] = jnp.zeros(n_scr.shape, F32)
        m_scr[...] = jnp.zeros(m_scr.shape, F32)

    x = qk_ref[...]
    xx_scr[8:8 + L, :] = x
    y = cb_ref[...]
    for j in range(C_CONV):
        y = y + cw_ref[j:j + 1, :] * xx_scr[8 - (C_CONV - 1) + j:8 - (C_CONV - 1) + j + L, :]
    xx_scr[0:8, :] = x[L - 8:L, :]
    qk = y * jax.nn.sigmoid(y)

    a = sm_ref[...] + fb_ref[...]
    logf = _log_sigmoid(a)
    row = lax.broadcasted_iota(jnp.int32, (L, L), 0)
    col = lax.broadcasted_iota(jnp.int32, (L, L), 1)
    causal = col <= row
    tri = jnp.where(causal, 1.0, 0.0).astype(F32)
    bcol = jnp.dot(tri, logf, preferred_element_type=F32, precision=lax.Precision.HIGHEST)
    lane = lax.broadcasted_iota(jnp.int32, (L, LANES), 1)
    mix = jnp.where(lane >= _FG_LANE, bcol, a)
    tr_scr[0:L, :] = mix
    tr_scr[L:2 * L, :] = jnp.zeros((L, LANES), F32)
    rows_t = jnp.transpose(tr_scr[...])

    for h in range(C_HEADS):
        sl = slice(h * d, (h + 1) * d)
        q = qk[:, sl] * (d ** -0.5)
        k = qk[:, C_HEADS * d + h * d:C_HEADS * d + (h + 1) * d]
        v = v_ref[:, sl]
        qb16 = q.astype(BF16)
        b_col = bcol[:, _FG_LANE + h:_FG_LANE + h + 1]
        i_col = a[:, _IG_LANE + h:_IG_LANE + h + 1]
        b_row = rows_t[_FG_LANE + h:_FG_LANE + h + 1, 0:L]
        i_row = rows_t[_IG_LANE + h:_IG_LANE + h + 1, 0:L]
        g = b_col[L - 1:L, :]
        m_prev = m_scr[h][:, 0:1]
        c_prev = c_scr[h]
        n_prev = n_scr[h]

        dmat = jnp.where(causal, b_col - b_row + i_row, NEG_BIG)
        m_inter = b_col + m_prev
        m_row = jnp.maximum(m_inter, jnp.max(dmat, axis=1, keepdims=True))
        inter = jnp.exp(m_inter - m_row)
        s_qk = lax.dot_general(qb16, k.astype(BF16), (((1,), (1,)), ((), ())), preferred_element_type=F32)
        s_qk = s_qk * jnp.exp(dmat - m_row)
        num = (jnp.dot(s_qk.astype(BF16), v, preferred_element_type=F32)
               + inter * jnp.dot(qb16, c_prev.astype(BF16), preferred_element_type=F32))
        den = jnp.sum(s_qk, axis=1, keepdims=True) + inter * jnp.sum(q * n_prev, axis=1, keepdims=True)
        hval = num / jnp.maximum(jnp.abs(den), jnp.exp(-m_row))
        out_ref[:, sl] = jax.nn.sigmoid(o_ref[:, sl]) * hval

        w_log = g - b_col + i_col
        m_loc = jnp.max(w_log, axis=0, keepdims=True)
        kw = k * jnp.exp(w_log - m_loc)
        kv = lax.dot_general(kw.astype(BF16), v, (((0,), (0,)), ((), ())), preferred_element_type=F32)
        ksum = jnp.sum(kw, axis=0, keepdims=True)
        m_new = jnp.maximum(g + m_prev, m_loc)
        fa = jnp.exp(g + m_prev - m_new)
        fb = jnp.exp(m_loc - m_new)
        c_scr[h] = fa * c_prev + fb * kv
        n_scr[h] = fa * n_prev + fb * ksum
        m_scr[h] = jnp.broadcast_to(m_new, (1, LANES))


def _mlstm(f32_out, bf_out, fbias_row, conv_w, conv_b, batch, seq):
    L = CHUNK
    nc = seq // L
    return pl.pallas_call(
        _mlstm_kernel,
        grid=(batch, nc),
        in_specs=[
            pl.BlockSpec((L, 1024), lambda b, c: (b * nc + c, 5)),
            pl.BlockSpec((L, 512), lambda b, c: (b * nc + c, 3)),
            pl.BlockSpec((L, 512), lambda b, c: (b * nc + c, 9)),
            pl.BlockSpec((L, LANES), lambda b, c: (b * nc + c, F32_W // LANES - 1)),
            pl.BlockSpec((1, LANES), lambda b, c: (0, 0)),
            pl.BlockSpec((C_CONV, 1024), lambda b, c: (0, 0)),
            pl.BlockSpec((1, 1024), lambda b, c: (0, 0)),
        ],
        out_specs=pl.BlockSpec((L, 512), lambda b, c: (b * nc + c, 0)),
        out_shape=jax.ShapeDtypeStruct((batch * seq, 512), F32),
        scratch_shapes=[
            pltpu.VMEM((8 + L, 1024), F32),
            pltpu.VMEM((C_HEADS, C_HEAD_DIM, C_HEAD_DIM), F32),
            pltpu.VMEM((C_HEADS, 1, C_HEAD_DIM), F32),
            pltpu.VMEM((C_HEADS, 1, LANES), F32),
            pltpu.VMEM((2 * L, LANES), F32),
        ],
        compiler_params=pltpu.CompilerParams(
            dimension_semantics=("arbitrary", "arbitrary"), vmem_limit_bytes=VMEM_LIMIT),
        name="mlstm",
    )(f32_out, bf_out, f32_out, f32_out, fbias_row, conv_w, conv_b)


def _silu(x):
    return x * jax.nn.sigmoid(x)


def _merge_kernel(x_ref, ya_ref, yb_ref, yc_ref, gate_ref, z_ref, cn_ref, wbr_ref, wout_ref, fg_ref,
                  o_ref, *, final):
    d = C_HEAD_DIM
    yc = yc_ref[...]
    parts = []
    for h in range(C_HEADS):
        hv = yc[:, h * d:(h + 1) * d]
        hc = hv - jnp.mean(hv, axis=-1, keepdims=True)
        parts.append(hc * lax.rsqrt(jnp.mean(hc * hc, axis=-1, keepdims=True) + NORM_EPS))
    yc_n = jnp.concatenate(parts, axis=1) * cn_ref[...]
    merged = None
    for n, y in enumerate((ya_ref[...], yb_ref[...], yc_n)):
        z = z_ref[:, n * BRANCH_WIDTH:(n + 1) * BRANCH_WIDTH]
        br = (y * _silu(z)).astype(BF16)
        proj = jnp.dot(br, wbr_ref[n], preferred_element_type=F32)
        term = jax.nn.sigmoid(gate_ref[:, n * D_MODEL:(n + 1) * D_MODEL]) * proj
        merged = term if merged is None else merged + term
    out = x_ref[...] + jnp.dot(merged.astype(BF16), wout_ref[...], preferred_element_type=F32)
    if final:
        ms = jnp.mean(out * out, axis=-1, keepdims=True)
        out = out * lax.rsqrt(ms + NORM_EPS) * fg_ref[...]
    o_ref[...] = out


def _merge(x2d, ya, yb, yc, f32_out, cn, wbr, wout, fg, final, tm=256):
    t, d = x2d.shape
    kern = functools.partial(_merge_kernel, final=final)
    return pl.pallas_call(
        kern,
        grid=(t // tm,),
        in_specs=[
            pl.BlockSpec((tm, d), lambda i: (i, 0)),
            pl.BlockSpec((tm, 512), lambda i: (i, 0)),
            pl.BlockSpec((tm, 512), lambda i: (i, 0)),
            pl.BlockSpec((tm, 512), lambda i: (i, 0)),
            pl.BlockSpec((tm, 3 * D_MODEL), lambda i: (i, 0)),
            pl.BlockSpec((tm, 3 * BRANCH_WIDTH), lambda i: (i, 2)),
            pl.BlockSpec((1, 512), lambda i: (0, 0)),
            pl.BlockSpec((N_BRANCH, BRANCH_WIDTH, d), lambda i: (0, 0, 0)),
            pl.BlockSpec((d, d), lambda i: (0, 0)),
            pl.BlockSpec((1, d), lambda i: (0, 0)),
        ],
        out_specs=pl.BlockSpec((tm, d), lambda i: (i, 0)),
        out_shape=jax.ShapeDtypeStruct((t, d), F32),
        compiler_params=pltpu.CompilerParams(
            dimension_semantics=("arbitrary",), vmem_limit_bytes=VMEM_LIMIT),
        name="merge",
    )(x2d, ya, yb, yc, f32_out, f32_out, cn, wbr, wout, fg)


def _layer(x2d, batch, seq, ctab, stab, norm_g, w_in, b_in, f_bias, conv_w, conv_b, rel_bias,
           c_norm_w, w_branch, w_out, final_g, final):
    g = norm_g[None, :]
    b_row = b_in[None, :]
    w_rope = _take_cols(w_in, _ROPE_NAMES, ROPE_W)
    b_rope = _take_cols(b_row, _ROPE_NAMES, ROPE_W)
    rope_out = _proj_rope(x2d, g, w_rope.astype(BF16), _rope_partner(w_rope).astype(BF16),
                          b_rope, _rope_partner(b_rope), ctab, stab, seq)
    bf_out, av_t = _proj_bf(x2d, g, _take_cols(w_in, _BF_NAMES, BF_W).astype(BF16),
                            _take_cols(b_row, _BF_NAMES, BF_W))
    f32_out = _proj(x2d, g, _take_cols(w_in, _F32_NAMES, F32_W).astype(BF16),
                    _take_cols(b_row, _F32_NAMES, F32_W), F32, F32_TN, tm=min(1024, seq))

    y_a = _dsa(rope_out, av_t, f32_out, batch, seq)
    y_b = _band(bf_out, _band_bias_table(rel_bias), batch, seq)
    fbias_row = jnp.zeros((1, LANES), F32).at[0, _FG_LANE:_FG_LANE + C_HEADS].set(f_bias)
    y_c = _mlstm(f32_out, bf_out, fbias_row, conv_w, conv_b[None, :], batch, seq)
    return _merge(x2d, y_a, y_b, y_c, f32_out, c_norm_w[None, :], w_branch.astype(BF16),
                  w_out.astype(BF16), final_g[None, :], final)


def kernel(x, norm_g, w_in, b_in, f_bias, conv_w, conv_b, rel_bias, c_norm_w, w_branch, w_out, final_g):
    batch, seq, d = x.shape
    ctab, stab = _rope_tables(seq)
    x2d = x.reshape(batch * seq, d)
    depth = norm_g.shape[0]
    for l in range(depth):
        x2d = _layer(x2d, batch, seq, ctab, stab, norm_g[l], w_in[l], b_in[l], f_bias[l], conv_w[l],
                     conv_b[l], rel_bias[l], c_norm_w[l], w_branch[l], w_out[l], final_g, l == depth - 1)
    return x2d.reshape(batch, seq, d)
```
